```python
import jax, jax.numpy as jnp
from jax import lax
import numpy as np

D_MODEL = 1024
BATCH = 4
SEQ = 4096
DEPTH = 4

MEM_LEN = 256
N_MIXERS = 2
N_GLA = (DEPTH + 1) // 2
N_SSD = DEPTH // 2
EPS = 1e-6
RESID_SCALE = (2 * DEPTH) ** -0.5

XA_HEADS = 4
XA_HEAD_DIM = 256
XA_WIDTH = XA_HEADS * XA_HEAD_DIM

GLA_HEADS = 4
GLA_DK = D_MODEL // 2
GLA_DV = D_MODEL
GLA_HEAD_K = GLA_DK // GLA_HEADS
GLA_HEAD_V = GLA_DV // GLA_HEADS
GLA_GATE_RANK = 16
GLA_GATE_TAU = 16.0
GLA_CHUNK = 64
GLA_IN = 2 * GLA_DK + 2 * GLA_DV + GLA_GATE_RANK + XA_WIDTH

SSD_EXPAND = 2
SSD_D_INNER = SSD_EXPAND * D_MODEL
SSD_HEAD_DIM = 64
SSD_HEADS = SSD_D_INNER // SSD_HEAD_DIM
SSD_GROUPS = 8
SSD_STATE = 128
SSD_CONV_K = 4
SSD_CONV_DIM = SSD_D_INNER + 2 * SSD_GROUPS * SSD_STATE
SSD_CHUNK = 128
SSD_IN = SSD_D_INNER + SSD_CONV_DIM + SSD_HEADS + XA_WIDTH

FFN_HIDDEN = ((8 * D_MODEL + 3 * 256 - 1) // (3 * 256)) * 256

kernel_name = "gla_ssd_memxattn_hybrid_trunk"


def rms_norm(x, w):
    xf = x.astype(jnp.float32)
    y = xf * lax.rsqrt(jnp.mean(xf * xf, axis=-1, keepdims=True) + EPS)
    return (y * w.astype(jnp.float32)).astype(x.dtype)


def chunk_scan(decay, inc):
    def step(s, xs):
        d, u = xs
        return d * s + u, s
    _, s_in = lax.scan(step, jnp.zeros_like(inc[0]), (decay, inc))
    return s_in


def gla_chunked(q, k, v, log_a):
    B_, S_, H, dk = q.shape
    dv = v.shape[-1]
    L = GLA_CHUNK
    N = S_ // L
    q = q.reshape(B_, N, L, H, dk)
    k = k.reshape(B_, N, L, H, dk)
    log_a = log_a.reshape(B_, N, L, H, dk)
    v = v.reshape(B_, N, L, H, dv)
    b = jnp.cumsum(log_a, axis=2)
    b_last = b[:, :, -1:]
    q_dec = q * jnp.exp(b)
    scores = jnp.einsum('bnlhd,bnmhd->bnhlm', q_dec, k * jnp.exp(-b))
    causal = jnp.tril(jnp.ones((L, L), dtype=bool))
    scores = jnp.where(causal, scores, 0.0)
    o_intra = jnp.einsum('bnhlm,bnmhv->bnlhv', scores, v)
    chunk_inc = jnp.einsum('bnlhd,bnlhv->nbhdv', k * jnp.exp(b_last - b), v)
    chunk_dec = jnp.exp(b_last[:, :, 0]).transpose(1, 0, 2, 3)[..., None]
    s_in = chunk_scan(chunk_dec, chunk_inc)
    o_inter = jnp.einsum('bnlhd,nbhdv->bnlhv', q_dec, s_in)
    return (o_intra + o_inter).reshape(B_, S_, H, dv)


def ssd_chunked(x, dt, a, b_mat, c_mat):
    B_, S_, H, P = x.shape
    G, Nst = b_mat.shape[2], b_mat.shape[3]
    R = H // G
    L = SSD_CHUNK
    C_ = S_ // L
    xdt = (x * dt[..., None]).reshape(B_, C_, L, G, R, P)
    a_dt = (dt * a).reshape(B_, C_, L, G, R).transpose(0, 1, 3, 4, 2)
    bm = b_mat.reshape(B_, C_, L, G, Nst)
    cm = c_mat.reshape(B_, C_, L, G, Nst)
    cs = jnp.cumsum(a_dt, axis=-1)
    causal = jnp.tril(jnp.ones((L, L), dtype=bool))
    seg = jnp.exp(jnp.where(causal, cs[..., :, None] - cs[..., None, :], -jnp.inf))
    cb = jnp.einsum('bclgn,bcmgn->bcglm', cm, bm)
    y_diag = jnp.einsum('bcglm,bcgrlm,bcmgrp->bclgrp', cb, seg, xdt)
    decay_to_end = jnp.exp(cs[..., -1:] - cs)
    chunk_inc = jnp.einsum('bclgn,bcgrl,bclgrp->cbgrpn', bm, decay_to_end, xdt)
    chunk_dec = jnp.exp(cs[..., -1]).transpose(1, 0, 2, 3)[..., None, None]
    s_in = chunk_scan(chunk_dec, chunk_inc)
    y_off = jnp.einsum('bclgn,cbgrpn,bcgrl->bclgrp', cm, s_in, jnp.exp(cs))
    return (y_diag + y_off).reshape(B_, S_, H, P)


def memory_attention(xq, mem_kv):
    B_, S_, _ = xq.shape
    q = xq.reshape(B_, S_, XA_HEADS, XA_HEAD_DIM)
    k, v = jnp.split(mem_kv, 2, axis=-1)
    k = k.reshape(B_, -1, XA_HEADS, XA_HEAD_DIM)
    v = v.reshape(B_, -1, XA_HEADS, XA_HEAD_DIM)
    s = jnp.einsum('bshd,bmhd->bhsm', q, k).astype(jnp.float32) * (XA_HEAD_DIM ** -0.5)
    p = jax.nn.softmax(s, axis=-1).astype(v.dtype)
    return jnp.einsum('bhsm,bmhd->bshd', p, v).reshape(B_, S_, XA_WIDTH)


def gla_mixer(hn, w_in, w_gate2, b_gate, head_norm):
    B_, S_, _ = hn.shape
    proj = hn @ w_in
    cuts = [GLA_DK, 2 * GLA_DK, 2 * GLA_DK + GLA_DV, 2 * GLA_DK + 2 * GLA_DV,
            2 * GLA_DK + 2 * GLA_DV + GLA_GATE_RANK]
    q, k, v, g, gate_lr, xq = jnp.split(proj, cuts, axis=-1)
    log_a = jax.nn.log_sigmoid((gate_lr @ w_gate2 + b_gate).astype(jnp.float32)) / GLA_GATE_TAU
    hk = lambda t: t.reshape(B_, S_, GLA_HEADS, GLA_HEAD_K).astype(jnp.float32)
    o = gla_chunked(hk(q) * (GLA_HEAD_K ** -0.5), hk(k),
                    v.reshape(B_, S_, GLA_HEADS, GLA_HEAD_V).astype(jnp.float32), hk(log_a))
    o = rms_norm(o, head_norm).reshape(B_, S_, GLA_DV).astype(hn.dtype)
    return o * jax.nn.silu(g), xq


def ssd_mixer(hn, w_in, conv_w, conv_b, dt_bias, a_log, d_skip, norm_w):
    B_, S_, _ = hn.shape
    proj = hn @ w_in
    cuts = [SSD_D_INNER, SSD_D_INNER + SSD_CONV_DIM, SSD_D_INNER + SSD_CONV_DIM + SSD_HEADS]
    z, xbc, dt_raw, xq = jnp.split(proj, cuts, axis=-1)
    xbc = lax.conv_general_dilated(
        xbc, conv_w[:, None, :], window_strides=(1,), padding=[(SSD_CONV_K - 1, 0)],
        dimension_numbers=('NWC', 'WIO', 'NWC'), feature_group_count=SSD_CONV_DIM) + conv_b
    xbc = jax.nn.silu(xbc)
    xs, bm, cm = jnp.split(xbc, [SSD_D_INNER, SSD_D_INNER + SSD_GROUPS * SSD_STATE], axis=-1)
    dt = jax.nn.softplus(dt_raw.astype(jnp.float32) + dt_bias.astype(jnp.float32))
    a = -jnp.exp(a_log.astype(jnp.float32))
    xh = xs.reshape(B_, S_, SSD_HEADS, SSD_HEAD_DIM).astype(jnp.float32)
    y = ssd_chunked(xh, dt, a,
                    bm.reshape(B_, S_, SSD_GROUPS, SSD_STATE).astype(jnp.float32),
                    cm.reshape(B_, S_, SSD_GROUPS, SSD_STATE).astype(jnp.float32))
    y = y + d_skip.astype(jnp.float32)[:, None] * xh
    y = y.reshape(B_, S_, SSD_D_INNER).astype(hn.dtype)
    return rms_norm(y * jax.nn.silu(z), norm_w), xq


def setup_inputs(seed: int = 0) -> dict:
    key = jax.random.key(seed)
    ks = jax.random.split(key, 24)
    f32 = jnp.float32
    nrm = lambda k, shape, scale: jax.random.normal(k, shape, f32) * scale
    gain = lambda k, shape: 1.0 + 0.02 * jax.random.normal(k, shape, f32)
    dt0 = jnp.exp(jax.random.uniform(ks[13], (N_SSD, SSD_HEADS), f32)
                  * (np.log(0.1) - np.log(0.001)) + np.log(0.001)).astype(f32)
    return {
        "x": nrm(ks[0], (BATCH, SEQ, D_MODEL), 1.0),
        "mem": nrm(ks[1], (BATCH, MEM_LEN, D_MODEL), 1.0),
        "mix_norm": gain(ks[2], (DEPTH, D_MODEL)),
        "mem_norm": gain(ks[3], (DEPTH, D_MODEL)),
        "w_mem_kv": nrm(ks[4], (DEPTH, D_MODEL, 2 * XA_WIDTH), D_MODEL ** -0.5),
        "gla_w_in": nrm(ks[5], (N_GLA, D_MODEL, GLA_IN), D_MODEL ** -0.5),
        "gla_w_gate2": nrm(ks[6], (N_GLA, GLA_GATE_RANK, GLA_DK), GLA_GATE_RANK ** -0.5),
        "gla_b_gate": nrm(ks[7], (N_GLA, GLA_DK), 0.1),
        "gla_head_norm": gain(ks[8], (N_GLA, GLA_HEAD_V)),
        "gla_w_out": nrm(ks[9], (N_GLA, GLA_DV + XA_WIDTH, D_MODEL), (GLA_DV + XA_WIDTH) ** -0.5 * RESID_SCALE),
        "ssd_w_in": nrm(ks[10], (N_SSD, D_MODEL, SSD_IN), D_MODEL ** -0.5),
        "ssd_conv_w": nrm(ks[11], (N_SSD, SSD_CONV_K, SSD_CONV_DIM), SSD_CONV_K ** -0.5),
        "ssd_conv_b": nrm(ks[12], (N_SSD, SSD_CONV_DIM), 0.02),
        "ssd_dt_bias": dt0 + jnp.log(-jnp.expm1(-dt0)),
        "ssd_a_log": jnp.log(jax.random.uniform(ks[14], (N_SSD, SSD_HEADS), f32, 1.0, 16.0)),
        "ssd_d": gain(ks[15], (N_SSD, SSD_HEADS)),
        "ssd_norm": gain(ks[16], (N_SSD, SSD_D_INNER)),
        "ssd_w_out": nrm(ks[17], (N_SSD, SSD_D_INNER + XA_WIDTH, D_MODEL), (SSD_D_INNER + XA_WIDTH) ** -0.5 * RESID_SCALE),
        "ffn_norm": gain(ks[18], (DEPTH, D_MODEL)),
        "ffn_w_in": nrm(ks[19], (DEPTH, D_MODEL, 2 * FFN_HIDDEN), D_MODEL ** -0.5),
        "ffn_w_out": nrm(ks[20], (DEPTH, FFN_HIDDEN, D_MODEL), FFN_HIDDEN ** -0.5 * RESID_SCALE),
        "final_norm": gain(ks[21], (D_MODEL,)),
    }


def reference(x, mem, mix_norm, mem_norm, w_mem_kv, gla_w_in, gla_w_gate2, gla_b_gate,
              gla_head_norm, gla_w_out, ssd_w_in, ssd_conv_w, ssd_conv_b, ssd_dt_bias,
              ssd_a_log, ssd_d, ssd_norm, ssd_w_out, ffn_norm, ffn_w_in, ffn_w_out, final_norm):
    h = x
    for i in range(DEPTH):
        j = i // N_MIXERS
        hn = rms_norm(h, mix_norm[i])
        if i % N_MIXERS == 0:
            mix, xq = gla_mixer(hn, gla_w_in[j], gla_w_gate2[j], gla_b_gate[j], gla_head_norm[j])
            w_out = gla_w_out[j]
        else:
            mix, xq = ssd_mixer(hn, ssd_w_in[j], ssd_conv_w[j], ssd_conv_b[j], ssd_dt_bias[j],
                                ssd_a_log[j], ssd_d[j], ssd_norm[j])
            w_out = ssd_w_out[j]
        mem_kv = rms_norm(mem, mem_norm[i]) @ w_mem_kv[i]
        xa = memory_attention(xq, mem_kv)
        h = h + jnp.concatenate([mix, xa], axis=-1) @ w_out
        hn = rms_norm(h, ffn_norm[i])
        gate, up = jnp.split(hn @ ffn_w_in[i], 2, axis=-1)
        h = h + (jax.nn.silu(gate) * up) @ ffn_w_out[i]
    return rms_norm(h, final_norm)
```

```python
import functools

import jax
import jax.numpy as jnp
from jax import lax
from jax.experimental import pallas as pl
from jax.experimental.pallas import tpu as pltpu

F32 = jnp.float32
BF16 = jnp.bfloat16
EPS = 1e-6

LANES = 128
XA_HEADS = 4
XA_HEAD_DIM = 256
XA_WIDTH = XA_HEADS * XA_HEAD_DIM
GLA_HEADS = 4
GLA_HEAD_K = 128
GLA_HEAD_V = 256
GLA_DK = GLA_HEADS * GLA_HEAD_K
GLA_DV = GLA_HEADS * GLA_HEAD_V
GLA_GATE_RANK = 16
GLA_GATE_TAU = 16.0
SSD_D_INNER = 2048
SSD_HEAD_DIM = 64
SSD_HEADS = 32
SSD_GROUPS = 8
SSD_STATE = 128
SSD_CONV_K = 4
SSD_BC = SSD_GROUPS * SSD_STATE
SSD_CONV_DIM = SSD_D_INNER + 2 * SSD_BC
CHUNK = 128
VMEM_LIMIT = 56 * 1024 * 1024

NT = (((1,), (1,)), ((), ()))
TN = (((0,), (0,)), ((), ()))


def _dot(a, b):
    return jnp.dot(a, b, preferred_element_type=F32)


def _dot_nt(a, b):
    return lax.dot_general(a, b, NT, preferred_element_type=F32)


def _dot_tn(a, b):
    return lax.dot_general(a, b, TN, preferred_element_type=F32)


def _rms(x, w):
    return x * lax.rsqrt(jnp.mean(x * x, axis=-1, keepdims=True) + EPS) * w


def _sigmoid(x):
    return 1.0 / (1.0 + jnp.exp(-x))


def _softplus(x):
    return jnp.maximum(x, 0.0) + jnp.log(1.0 + jnp.exp(-jnp.abs(x)))


def _causal(n):
    row = lax.broadcasted_iota(jnp.int32, (n, n), 0)
    col = lax.broadcasted_iota(jnp.int32, (n, n), 1)
    return row >= col


def _cumsum_rows(tril, x):
    x1 = x.astype(BF16)
    r1 = x - x1.astype(F32)
    x2 = r1.astype(BF16)
    x3 = (r1 - x2.astype(F32)).astype(BF16)
    return _dot(tril, x1) + _dot(tril, x2) + _dot(tril, x3)


def _params(sem):
    return pltpu.CompilerParams(dimension_semantics=sem, vmem_limit_bytes=VMEM_LIMIT)


def _resident(shape):
    nd = len(shape)
    return pl.BlockSpec(shape, lambda *_: (0,) * nd, pipeline_mode=pl.Buffered(1))


def _inproj_kernel(x_ref, nw_ref, w_ref, main_ref, aux_ref, *, n_main, tn):
    xn = _rms(x_ref[...], nw_ref[...]).astype(BF16)
    for c in range(n_main // tn):
        cs = slice(c * tn, (c + 1) * tn)
        main_ref[:, cs] = _dot(xn, w_ref[:, cs]).astype(BF16)
    aux_ref[...] = _dot(xn, w_ref[:, n_main:])


def _inproj(h, norm_w, w, n_main, tm=512, tn=512):
    t, d = h.shape
    n = w.shape[1]
    return pl.pallas_call(
        functools.partial(_inproj_kernel, n_main=n_main, tn=tn),
        grid=(t // tm,),
        in_specs=[pl.BlockSpec((tm, d), lambda i: (i, 0)),
                  _resident((1, d)),
                  _resident((d, n))],
        out_specs=[pl.BlockSpec((tm, n_main), lambda i: (i, 0)),
                   pl.BlockSpec((tm, n - n_main), lambda i: (i, 0))],
        out_shape=[jax.ShapeDtypeStruct((t, n_main), BF16),
                   jax.ShapeDtypeStruct((t, n - n_main), F32)],
        compiler_params=_params(("parallel",)),
        name="inproj",
    )(h, norm_w, w)


def _memkv_kernel(mem_ref, nw_ref, w_ref, o_ref):
    xn = _rms(mem_ref[...], nw_ref[0]).astype(BF16)
    o_ref[0] = _dot(xn, w_ref[0]).astype(BF16)


def _memkv(mem2d, mem_norm, w_kv):
    depth, d, n = w_kv.shape
    rows = mem2d.shape[0]
    return pl.pallas_call(
        _memkv_kernel,
        grid=(depth,),
        in_specs=[_resident((rows, d)),
                  pl.BlockSpec((1, 1, d), lambda i: (i, 0, 0)),
                  pl.BlockSpec((1, d, n), lambda i: (i, 0, 0))],
        out_specs=pl.BlockSpec((1, rows, n), lambda i: (i, 0, 0)),
        out_shape=jax.ShapeDtypeStruct((depth, rows, n), BF16),
        compiler_params=_params(("parallel",)),
        name="memkv",
    )(mem2d, mem_norm.reshape(depth, 1, d), w_kv)


def _gla_kernel(q_ref, k_ref, v_ref, g_ref, gl_ref, wg_ref, bg_ref, hn_ref, o_ref, st_ref):
    L = CHUNK

    @pl.when(pl.program_id(1) == 0)
    def _():
        st_ref[...] = jnp.zeros_like(st_ref)

    causal = _causal(L)
    tril = jnp.where(causal, 1.0, 0.0).astype(BF16)
    pre = _dot(gl_ref[...].astype(BF16), wg_ref[...]) + bg_ref[...]
    log_a = (jnp.minimum(pre, 0.0) - jnp.log(1.0 + jnp.exp(-jnp.abs(pre)))) * (1.0 / GLA_GATE_TAU)
    b = _cumsum_rows(tril, log_a)
    b_mid = b[L // 2 - 1:L // 2, :]
    b_last = b[L - 1:L, :]
    q = q_ref[...].astype(F32) * (GLA_HEAD_K ** -0.5) * jnp.exp(b - b_mid)
    k = k_ref[...].astype(F32) * jnp.exp(b_mid - b)
    q_mid = q.astype(BF16)
    k_mid = k.astype(BF16)
    q_dec = (q * jnp.exp(b_mid)).astype(BF16)
    k_end = (k * jnp.exp(b_last - b_mid)).astype(BF16)
    dec = jnp.exp(b_last)
    for h in range(GLA_HEADS):
        ks = slice(h * GLA_HEAD_K, (h + 1) * GLA_HEAD_K)
        vs = slice(h * GLA_HEAD_V, (h + 1) * GLA_HEAD_V)
        s = jnp.where(causal, _dot_nt(q_mid[:, ks], k_mid[:, ks]), 0.0).astype(BF16)
        v = v_ref[:, vs]
        st = st_ref[h]
        o = _dot(s, v) + _dot_nt(q_dec[:, ks], st.astype(BF16))
        st_ref[h] = st * dec[:, ks] + _dot_tn(v, k_end[:, ks])
        g = g_ref[:, vs].astype(F32)
        o_ref[:, vs] = (_rms(o, hn_ref[...]) * (g * _sigmoid(g))).astype(BF16)


def _gla(proj, aux, wg, bg, hnorm, batch, seq):
    t = proj.shape[0]
    nc = seq // CHUNK
    row = lambda b, c: b * nc + c
    return pl.pallas_call(
        _gla_kernel,
        grid=(batch, nc),
        in_specs=[pl.BlockSpec((CHUNK, GLA_DK), lambda b, c: (row(b, c), 0)),
                  pl.BlockSpec((CHUNK, GLA_DK), lambda b, c: (row(b, c), 1)),
                  pl.BlockSpec((CHUNK, GLA_DV), lambda b, c: (row(b, c), 1)),
                  pl.BlockSpec((CHUNK, GLA_DV), lambda b, c: (row(b, c), 2)),
                  pl.BlockSpec((CHUNK, LANES), lambda b, c: (row(b, c), 0)),
                  _resident(wg.shape), _resident(bg.shape), _resident(hnorm.shape)],
        out_specs=pl.BlockSpec((CHUNK, GLA_DV), lambda b, c: (row(b, c), 0)),
        out_shape=jax.ShapeDtypeStruct((t, GLA_DV), BF16),
        scratch_shapes=[pltpu.VMEM((GLA_HEADS, GLA_HEAD_V, GLA_HEAD_K), F32)],
        compiler_params=_params(("parallel", "arbitrary")),
        name="gla",
    )(proj, proj, proj, proj, aux, wg, bg, hnorm)


CONV_COLS = 512


def _ssd_kernel(z_ref, xs_ref, bm_ref, cm_ref, dt_ref, cw_ref, cb_ref, dtb_ref, alog_ref, dsk_ref,
                nw_ref, o_ref, st_ref, tail_ref, xs_s, bm_s, cm_s, y_s):
    L = CHUNK
    P = SSD_HEAD_DIM
    N = SSD_STATE

    @pl.when(pl.program_id(1) == 0)
    def _():
        st_ref[...] = jnp.zeros_like(st_ref)
        tail_ref[...] = jnp.zeros_like(tail_ref)

    rows8 = lax.broadcasted_iota(jnp.int32, (8, CONV_COLS), 0)

    def conv_silu(src_ref, dst_ref, c_base):
        for c in range(src_ref.shape[1] // CONV_COLS):
            src = slice(c * CONV_COLS, (c + 1) * CONV_COLS)
            cc = slice(c_base + c * CONV_COLS, c_base + (c + 1) * CONV_COLS)
            x = src_ref[:, src].astype(F32)
            tail = tail_ref[:, cc]
            acc = x * cw_ref[SSD_CONV_K - 1:SSD_CONV_K, cc] + cb_ref[:, cc]
            for s in range(1, SSD_CONV_K):
                xr = pltpu.roll(x, s, 0)
                first = jnp.where(rows8 < s, pltpu.roll(tail, s, 0), xr[0:8, :])
                shifted = jnp.concatenate([first, xr[8:, :]], axis=0)
                acc = acc + shifted * cw_ref[SSD_CONV_K - 1 - s:SSD_CONV_K - s, cc]
            tail_ref[:, cc] = x[L - 8:L, :]
            dst_ref[:, src] = acc * _sigmoid(acc)

    conv_silu(xs_ref, xs_s, 0)
    conv_silu(bm_ref, bm_s, SSD_D_INNER)
    conv_silu(cm_ref, cm_s, SSD_D_INNER + SSD_BC)

    causal = _causal(L)
    tril = jnp.where(causal, 1.0, 0.0).astype(BF16)
    dt = _softplus(dt_ref[...] + dtb_ref[...])
    a = -jnp.exp(alog_ref[...])
    cs = _cumsum_rows(tril, dt * a)
    cs_t = cs.T
    dt_t = dt.T
    w_t = jnp.exp(cs_t[:, L - 1:L] - cs_t) * dt_t
    dec_last = jnp.exp(cs[L - 1:L, :])
    lane_lo = lax.broadcasted_iota(jnp.int32, (1, LANES), 1) < P

    for g in range(SSD_GROUPS):
        gs = slice(g * N, (g + 1) * N)
        cm_g = cm_s[:, gs]
        bm_g = bm_s[:, gs]
        cb = _dot_nt(cm_g.astype(BF16), bm_g.astype(BF16))
        bm_gt = bm_g.T
        for e2 in range(2):
            ps = slice((2 * g + e2) * LANES, (2 * g + e2 + 1) * LANES)
            ss = slice(e2 * LANES, (e2 + 1) * LANES)
            x_f = xs_s[:, ps]
            x_b = x_f.astype(BF16)
            st = st_ref[g, :, ss]
            st_b = st.astype(BF16)
            ys, incs, decs = [], [], []
            for e in range(2):
                h = 4 * g + 2 * e2 + e
                col = cs[:, h:h + 1]
                seg = jnp.exp(col - cs_t[h:h + 1, :])
                m = jnp.where(causal, cb * seg * dt_t[h:h + 1, :], 0.0).astype(BF16)
                cm_h = (cm_g * jnp.exp(col)).astype(BF16)
                bm_w = (bm_gt * w_t[h:h + 1, :]).astype(BF16)
                ys.append(_dot(m, x_b) + _dot(cm_h, st_b))
                incs.append(_dot(bm_w, x_b))
                decs.append(dec_last[:, h:h + 1])
            y = jnp.where(lane_lo, ys[0], ys[1])
            inc = jnp.where(lane_lo, incs[0], incs[1])
            dec = jnp.where(lane_lo, decs[0], decs[1])
            st_ref[g, :, ss] = st * dec + inc
            y_s[:, ps] = y + dsk_ref[:, ps] * x_f

    z = z_ref[...].astype(F32)
    o_ref[...] = _rms(y_s[...] * (z * _sigmoid(z)), nw_ref[...]).astype(BF16)


def _ssd(proj, aux, conv_w, conv_b, dt_bias, a_log, d_skip, norm_w, batch, seq):
    t = proj.shape[0]
    nc = seq // CHUNK
    row = lambda b, c: b * nc + c
    return pl.pallas_call(
        _ssd_kernel,
        grid=(batch, nc),
        in_specs=[pl.BlockSpec((CHUNK, SSD_D_INNER), lambda b, c: (row(b, c), 0)),
                  pl.BlockSpec((CHUNK, SSD_D_INNER), lambda b, c: (row(b, c), 1)),
                  pl.BlockSpec((CHUNK, SSD_BC), lambda b, c: (row(b, c), 4)),
                  pl.BlockSpec((CHUNK, SSD_BC), lambda b, c: (row(b, c), 5)),
                  pl.BlockSpec((CHUNK, LANES), lambda b, c: (row(b, c), 0)),
                  _resident(conv_w.shape), _resident(conv_b.shape), _resident(dt_bias.shape),
                  _resident(a_log.shape), _resident(d_skip.shape), _resident(norm_w.shape)],
        out_specs=pl.BlockSpec((CHUNK, SSD_D_INNER), lambda b, c: (row(b, c), 0)),
        out_shape=jax.ShapeDtypeStruct((t, SSD_D_INNER), BF16),
        scratch_shapes=[pltpu.VMEM((SSD_GROUPS, SSD_STATE, 4 * SSD_HEAD_DIM), F32),
                        pltpu.VMEM((8, SSD_CONV_DIM), F32),
                        pltpu.VMEM((CHUNK, SSD_D_INNER), F32),
                        pltpu.VMEM((CHUNK, SSD_BC), F32),
                        pltpu.VMEM((CHUNK, SSD_BC), F32),
                        pltpu.VMEM((CHUNK, SSD_D_INNER), F32)],
        compiler_params=_params(("parallel", "arbitrary")),
        name="ssd",
    )(proj, proj, proj, proj, aux, conv_w, conv_b, dt_bias, a_log, d_skip, norm_w)


def _xattn_out_kernel(mix_ref, xq_ref, kv_ref, wo_ref, h_ref, o_ref, *, d_mix):
    acc = h_ref[...] + _dot(mix_ref[...], wo_ref[0:d_mix, :])
    for hd in range(XA_HEADS):
        ks = slice(hd * XA_HEAD_DIM, (hd + 1) * XA_HEAD_DIM)
        vs = slice(XA_WIDTH + hd * XA_HEAD_DIM, XA_WIDTH + (hd + 1) * XA_HEAD_DIM)
        s = _dot_nt(xq_ref[:, ks], kv_ref[0, :, ks]) * (XA_HEAD_DIM ** -0.5)
        p = jnp.exp(s - jnp.max(s, axis=-1, keepdims=True))
        pv = _dot(p.astype(BF16), kv_ref[0, :, vs]) / jnp.sum(p, axis=-1, keepdims=True)
        acc = acc + _dot(pv.astype(BF16), wo_ref[d_mix + hd * XA_HEAD_DIM:d_mix + (hd + 1) * XA_HEAD_DIM, :])
    o_ref[...] = acc


def _xattn_out(mix, proj, xq_block, memkv, layer, w_out, h, seq, tm=512):
    t, d = h.shape
    d_mix = mix.shape[1]
    mem_len = memkv.shape[1] // (t // seq)
    per_batch = seq // tm
    return pl.pallas_call(
        functools.partial(_xattn_out_kernel, d_mix=d_mix),
        grid=(t // tm,),
        in_specs=[pl.BlockSpec((tm, d_mix), lambda i: (i, 0)),
                  pl.BlockSpec((tm, XA_WIDTH), lambda i: (i, xq_block)),
                  pl.BlockSpec((1, mem_len, 2 * XA_WIDTH), lambda i: (layer, i // per_batch, 0)),
                  _resident(w_out.shape),
                  pl.BlockSpec((tm, d), lambda i: (i, 0))],
        out_specs=pl.BlockSpec((tm, d), lambda i: (i, 0)),
        out_shape=jax.ShapeDtypeStruct((t, d), F32),
        compiler_params=_params(("parallel",)),
        name="xattn_out",
    )(mix, proj, memkv, w_out, h)


def _ffn_kernel(h_ref, nw_ref, wi_ref, wo_ref, fnw_ref, o_ref, *, hidden, th, final):
    h = h_ref[...]
    xn = _rms(h, nw_ref[...]).astype(BF16)
    acc = h
    for c in range(hidden // th):
        gate = _dot(xn, wi_ref[:, c * th:(c + 1) * th])
        up = _dot(xn, wi_ref[:, hidden + c * th:hidden + (c + 1) * th])
        act = (gate * _sigmoid(gate) * up).astype(BF16)
        acc = acc + _dot(act, wo_ref[c * th:(c + 1) * th, :])
    if final:
        acc = _rms(acc, fnw_ref[...])
    o_ref[...] = acc


def _ffn(h, norm_w, w_in, w_out, final_w, final, tm=512, th=256):
    t, d = h.shape
    hidden = w_out.shape[0]
    return pl.pallas_call(
        functools.partial(_ffn_kernel, hidden=hidden, th=th, final=final),
        grid=(t // tm,),
        in_specs=[pl.BlockSpec((tm, d), lambda i: (i, 0)),
                  _resident((1, d)), _resident(w_in.shape), _resident(w_out.shape), _resident((1, d))],
        out_specs=pl.BlockSpec((tm, d), lambda i: (i, 0)),
        out_shape=jax.ShapeDtypeStruct((t, d), F32),
        compiler_params=_params(("parallel",)),
        name="ffn",
    )(h, norm_w, w_in, w_out, final_w)


def _pad_cols(w, n):
    return jnp.pad(w, ((0, 0), (0, n - w.shape[1])))


def _row(v, n=None):
    v = v.reshape(1, -1)
    return v if n is None else _pad_cols(v, n)


def kernel(x, mem, mix_norm, mem_norm, w_mem_kv, gla_w_in, gla_w_gate2, gla_b_gate, gla_head_norm, gla_w_out, ssd_w_in, ssd_conv_w, ssd_conv_b, ssd_dt_bias, ssd_a_log, ssd_d, ssd_norm, ssd_w_out, ffn_norm, ffn_w_in, ffn_w_out, final_norm):
    batch, seq, d = x.shape
    depth = mix_norm.shape[0]
    t = batch * seq
    h = x.reshape(t, d)
    memkv = _memkv(mem.reshape(-1, d), mem_norm, w_mem_kv.astype(BF16))
    final_w = _row(final_norm)

    gla_main = 2 * GLA_DK + 2 * GLA_DV
    ssd_main = SSD_D_INNER + SSD_CONV_DIM
    for i in range(depth):
        j = i // 2
        if i % 2 == 0:
            w = gla_w_in[j]
            w = jnp.concatenate([w[:, :gla_main], w[:, gla_main + GLA_GATE_RANK:],
                                 _pad_cols(w[:, gla_main:gla_main + GLA_GATE_RANK], LANES)], axis=1)
            n_main = gla_main + XA_WIDTH
            proj, aux = _inproj(h, _row(mix_norm[i]), w.astype(BF16), n_main)
            wg = jnp.pad(gla_w_gate2[j], ((0, LANES - GLA_GATE_RANK), (0, 0))).astype(BF16)
            mix = _gla(proj, aux, wg, _row(gla_b_gate[j]), _row(gla_head_norm[j]), batch, seq)
            w_out = gla_w_out[j]
        else:
            w = ssd_w_in[j]
            w = jnp.concatenate([w[:, :ssd_main], w[:, ssd_main + SSD_HEADS:],
                                 _pad_cols(w[:, ssd_main:ssd_main + SSD_HEADS], LANES)], axis=1)
            n_main = ssd_main + XA_WIDTH
            proj, aux = _inproj(h, _row(mix_norm[i]), w.astype(BF16), n_main)
            mix = _ssd(proj, aux, ssd_conv_w[j], _row(ssd_conv_b[j]), _row(ssd_dt_bias[j], LANES),
                       _row(ssd_a_log[j], LANES), _row(jnp.repeat(ssd_d[j], SSD_HEAD_DIM)),
                       _row(ssd_norm[j]), batch, seq)
            w_out = ssd_w_out[j]
        xq_block = (n_main - XA_WIDTH) // XA_WIDTH
        h = _xattn_out(mix, proj, xq_block, memkv, i, w_out.astype(BF16), h, seq)
        h = _ffn(h, _row(ffn_norm[i]), ffn_w_in[i].astype(BF16), ffn_w_out[i].astype(BF16),
                 final_w, final=(i == depth - 1))
    return h.reshape(batch, seq, d)
```

```python
import functools

import jax
import jax.numpy as jnp
from jax import lax
from jax.experimental import pallas as pl
from jax.experimental.pallas import tpu as pltpu

F32 = jnp.float32
BF16 = jnp.bfloat16
EPS = 1e-6
LOG2E = 1.4426950408889634
EXP2_MAX = 126.0

LANES = 128
BF16_ROWS = 16
XA_HEADS = 4
XA_HEAD_DIM = 256
XA_WIDTH = XA_HEADS * XA_HEAD_DIM
GLA_HEADS = 4
GLA_HEAD_K = 128
GLA_HEAD_V = 256
GLA_DK = GLA_HEADS * GLA_HEAD_K
GLA_DV = GLA_HEADS * GLA_HEAD_V
GLA_GATE_RANK = 16
GLA_GATE_TAU = 16.0
SSD_D_INNER = 2048
SSD_HEAD_DIM = 64
SSD_HEADS = 32
SSD_GROUPS = 8
SSD_STATE = 128
SSD_CONV_K = 4
SSD_BC = SSD_GROUPS * SSD_STATE
SSD_CONV_DIM = SSD_D_INNER + 2 * SSD_BC
CHUNK = 128
CONV_ROWS = 64
VMEM_LIMIT = 56 * 1024 * 1024

NT = (((1,), (1,)), ((), ()))
TN = (((0,), (0,)), ((), ()))


def _dot(a, b):
    return jnp.dot(a, b, preferred_element_type=F32)


def _dot_nt(a, b):
    return lax.dot_general(a, b, NT, preferred_element_type=F32)


def _dot_tn(a, b):
    return lax.dot_general(a, b, TN, preferred_element_type=F32)


def _rms(x, w):
    return x * lax.rsqrt(jnp.mean(x * x, axis=-1, keepdims=True) + EPS) * w


def _silu(x):
    hx = 0.5 * x
    return hx + hx * jnp.tanh(hx)


def _softplus(x):
    return jnp.maximum(x, 0.0) + jnp.log(1.0 + jnp.exp(-jnp.abs(x)))


def _causal(n):
    row = lax.broadcasted_iota(jnp.int32, (n, n), 0)
    col = lax.broadcasted_iota(jnp.int32, (n, n), 1)
    return row >= col


def _cumsum_rows(tril, x):
    x1 = x.astype(BF16)
    r1 = x - x1.astype(F32)
    x2 = r1.astype(BF16)
    x3 = (r1 - x2.astype(F32)).astype(BF16)
    return _dot(tril, x1) + _dot(tril, x2) + _dot(tril, x3)


def _params(sem):
    return pltpu.CompilerParams(dimension_semantics=sem, vmem_limit_bytes=VMEM_LIMIT)


def _resident(shape):
    nd = len(shape)
    return pl.BlockSpec(shape, lambda *_: (0,) * nd, pipeline_mode=pl.Buffered(1))


def _layer(shape, layer):
    nd = len(shape)
    return pl.BlockSpec((1,) + tuple(shape[1:]), lambda *_: (layer,) + (0,) * (nd - 1),
                        pipeline_mode=pl.Buffered(1))


def _inproj_kernel(x_ref, nw_ref, w_ref, main_ref, aux_ref, *, n_main, tn):
    xn = _rms(x_ref[...], nw_ref[0]).astype(BF16)
    for c in range(n_main // tn):
        cs = slice(c * tn, (c + 1) * tn)
        main_ref[:, cs] = _dot(xn, w_ref[0, :, cs]).astype(BF16)
    aux_ref[...] = _dot(xn, w_ref[0, :, n_main:])


def _inproj_conv_kernel(x_ref, xp_ref, nw_ref, w_ref, cw_ref, cb_ref, main_ref, aux_ref, *,
                        n_main, conv_lo, conv_hi, tn, blocks_per_seq):
    tm = x_ref.shape[0]
    hist = xp_ref.shape[0]
    xn = _rms(x_ref[...], nw_ref[0]).astype(BF16)
    keep = (pl.program_id(0) % blocks_per_seq != 0).astype(F32)
    xp = (_rms(xp_ref[...], nw_ref[0]) * keep).astype(BF16)
    lhs = jnp.concatenate([xp, xn], axis=0)
    sub = lax.broadcasted_iota(jnp.int32, (1, 8, tn), 1)
    is_conv = lambda c: conv_lo <= c * tn < conv_hi
    conv_chunks = [c for c in range(n_main // tn) if is_conv(c)]
    plain_chunks = [c for c in range(n_main // tn) if not is_conv(c)]
    order = []
    for k in range(max(len(conv_chunks), len(plain_chunks))):
        order += conv_chunks[k:k + 1] + plain_chunks[k:k + 1]
    for c in order:
        cs = slice(c * tn, (c + 1) * tn)
        if is_conv(c):
            cc = slice(c * tn - conv_lo, (c + 1) * tn - conv_lo)
            r = _dot(lhs, w_ref[0, :, cs]).reshape((hist + tm) // 8, 8, tn)
            hw = 0.5 * cw_ref[0, :, cc]
            hb = 0.5 * cb_ref[0, :, cc]
            for k in range(tm // CONV_ROWS):
                t0 = hist // 8 + k * (CONV_ROWS // 8)
                win = r[t0 - 1:t0 + CONV_ROWS // 8]
                acc = hb + win[1:] * hw[SSD_CONV_K - 1:SSD_CONV_K, :]
                for s in range(1, SSD_CONV_K):
                    rr = pltpu.roll(win, s, 1)
                    acc = acc + jnp.where(sub < s, rr[:-1], rr[1:]) * hw[SSD_CONV_K - 1 - s:SSD_CONV_K - s, :]
                act = acc + acc * jnp.tanh(acc)
                main_ref[k * CONV_ROWS:(k + 1) * CONV_ROWS, cs] = act.reshape(CONV_ROWS, tn).astype(BF16)
        else:
            main_ref[:, cs] = _dot(xn, w_ref[0, :, cs]).astype(BF16)
    aux_ref[...] = _dot(xn, w_ref[0, :, n_main:])


def _inproj(h, norm_w, w, layer, wlayer, n_main, conv=None, seq=None, tm=512, tn=256):
    t, d = h.shape
    n = w.shape[2]
    out_specs = [pl.BlockSpec((tm, n_main), lambda i: (i, 0)),
                 pl.BlockSpec((tm, n - n_main), lambda i: (i, 0))]
    out_shape = [jax.ShapeDtypeStruct((t, n_main), BF16),
                 jax.ShapeDtypeStruct((t, n - n_main), F32)]
    x_spec = pl.BlockSpec((tm, d), lambda i: (i, 0))
    if conv is None:
        return pl.pallas_call(
            functools.partial(_inproj_kernel, n_main=n_main, tn=tn),
            grid=(t // tm,),
            in_specs=[x_spec, _layer(norm_w.shape, layer), _layer(w.shape, wlayer)],
            out_specs=out_specs, out_shape=out_shape,
            compiler_params=_params(("parallel",)),
            name="inproj",
        )(h, norm_w, w)
    conv_w, conv_b, conv_lo = conv
    per = tm // BF16_ROWS
    prev_spec = pl.BlockSpec((BF16_ROWS, d), lambda i: (jnp.maximum(i * per - 1, 0), 0))
    return pl.pallas_call(
        functools.partial(_inproj_conv_kernel, n_main=n_main, conv_lo=conv_lo,
                          conv_hi=conv_lo + conv_w.shape[2], tn=tn, blocks_per_seq=seq // tm),
        grid=(t // tm,),
        in_specs=[x_spec, prev_spec, _layer(norm_w.shape, layer), _layer(w.shape, wlayer),
                  _layer(conv_w.shape, wlayer), _layer(conv_b.shape, wlayer)],
        out_specs=out_specs, out_shape=out_shape,
        compiler_params=_params(("parallel",)),
        name="inproj_conv",
    )(h, h, norm_w, w, conv_w, conv_b)


def _memkv_kernel(mem_ref, nw_ref, w_ref, o_ref):
    xn = _rms(mem_ref[...], nw_ref[0]).astype(BF16)
    o_ref[0] = _dot(xn, w_ref[0]).astype(BF16)


def _memkv(mem2d, mem_norm, w_kv):
    depth, d, n = w_kv.shape
    rows = mem2d.shape[0]
    return pl.pallas_call(
        _memkv_kernel,
        grid=(depth,),
        in_specs=[_resident((rows, d)),
                  pl.BlockSpec((1, 1, d), lambda i: (i, 0, 0)),
                  pl.BlockSpec((1, d, n), lambda i: (i, 0, 0))],
        out_specs=pl.BlockSpec((1, rows, n), lambda i: (i, 0, 0)),
        out_shape=jax.ShapeDtypeStruct((depth, rows, n), BF16),
        compiler_params=_params(("parallel",)),
        name="memkv",
    )(mem2d, mem_norm, w_kv)


def _gla_kernel(q_ref, k_ref, v_ref, g_ref, gl_ref, wg_ref, bg_ref, hn_ref, o_ref, st_ref):
    L = CHUNK

    @pl.when(pl.program_id(1) == 0)
    def _():
        st_ref[...] = jnp.zeros_like(st_ref)

    causal = _causal(L)
    tril = jnp.where(causal, 1.0, 0.0).astype(BF16)
    for bi in range(q_ref.shape[0]):
        pre = _dot(gl_ref[bi].astype(BF16), wg_ref[0]) + bg_ref[0]
        log_a = (jnp.minimum(pre, 0.0) - jnp.log(1.0 + jnp.exp(-jnp.abs(pre)))) * (1.0 / GLA_GATE_TAU)
        b = _cumsum_rows(tril, log_a)
        b_mid = b[L // 2 - 1:L // 2, :]
        b_last = b[L - 1:L, :]
        q = q_ref[bi].astype(F32) * (GLA_HEAD_K ** -0.5) * jnp.exp(b - b_mid)
        k = k_ref[bi].astype(F32) * jnp.exp(b_mid - b)
        q_mid = q.astype(BF16)
        k_mid = k.astype(BF16)
        q_dec = (q * jnp.exp(b_mid)).astype(BF16)
        k_end = (k * jnp.exp(b_last - b_mid)).astype(BF16)
        dec = jnp.exp(b_last)
        for h in range(GLA_HEADS):
            ks = slice(h * GLA_HEAD_K, (h + 1) * GLA_HEAD_K)
            vs = slice(h * GLA_HEAD_V, (h + 1) * GLA_HEAD_V)
            s = jnp.where(causal, _dot_nt(q_mid[:, ks], k_mid[:, ks]), 0.0).astype(BF16)
            v = v_ref[bi, :, vs]
            st = st_ref[bi * GLA_HEADS + h]
            o = _dot(s, v) + _dot_nt(q_dec[:, ks], st.astype(BF16))
            st_ref[bi * GLA_HEADS + h] = st * dec[:, ks] + _dot_tn(v, k_end[:, ks])
            g = g_ref[bi, :, vs].astype(F32)
            o_ref[bi, :, vs] = (_rms(o, hn_ref[0]) * _silu(g)).astype(BF16)


def _gla(proj, aux, wg, bg, hnorm, layer, batch, seq):
    t, n = proj.shape
    nb = 4 if batch % 4 == 0 else 1
    proj3 = proj.reshape(batch, seq, n)
    aux3 = aux.reshape(batch, seq, LANES)
    out = pl.pallas_call(
        _gla_kernel,
        grid=(batch // nb, seq // CHUNK),
        in_specs=[pl.BlockSpec((nb, CHUNK, GLA_DK), lambda b, c: (b, c, 0)),
                  pl.BlockSpec((nb, CHUNK, GLA_DK), lambda b, c: (b, c, 1)),
                  pl.BlockSpec((nb, CHUNK, GLA_DV), lambda b, c: (b, c, 1)),
                  pl.BlockSpec((nb, CHUNK, GLA_DV), lambda b, c: (b, c, 2)),
                  pl.BlockSpec((nb, CHUNK, LANES), lambda b, c: (b, c, 0)),
                  _layer(wg.shape, layer), _layer(bg.shape, layer), _layer(hnorm.shape, layer)],
        out_specs=pl.BlockSpec((nb, CHUNK, GLA_DV), lambda b, c: (b, c, 0)),
        out_shape=jax.ShapeDtypeStruct((batch, seq, GLA_DV), BF16),
        scratch_shapes=[pltpu.VMEM((nb * GLA_HEADS, GLA_HEAD_V, GLA_HEAD_K), F32)],
        compiler_params=_params(("parallel", "arbitrary")),
        name="gla",
    )(proj3, proj3, proj3, proj3, aux3, wg, bg, hnorm)
    return out.reshape(t, GLA_DV)


def _ssd_kernel(xs_ref, bm_ref, cm_ref, dt_ref, dtb_ref, alog_ref, dsk_ref, o_ref, st_ref):
    L = CHUNK
    P = SSD_HEAD_DIM
    N = SSD_STATE

    @pl.when(pl.program_id(1) == 0)
    def _():
        st_ref[...] = jnp.zeros_like(st_ref)

    causal = _causal(L)
    tril = jnp.where(causal, 1.0, 0.0).astype(BF16)
    dt = _softplus(dt_ref[...] + dtb_ref[0])
    a = -jnp.exp(alog_ref[0])
    cs = _cumsum_rows(tril, dt * a)
    cs2 = cs * LOG2E
    u2 = (cs - jnp.log(dt)) * LOG2E
    u2_t = u2.T
    last2 = cs2[L - 1:L, :]
    dec_last = jnp.exp2(last2)
    e_cs = jnp.exp2(cs2)
    w_end = jnp.exp2(last2 - u2)
    lane_lo = lax.broadcasted_iota(jnp.int32, (1, LANES), 1) < P

    def pair_cols(arr, h0):
        return jnp.where(lane_lo, arr[:, h0:h0 + 1], arr[:, h0 + 1:h0 + 2])

    for g in range(SSD_GROUPS):
        gs = slice(g * N, (g + 1) * N)
        cm_g = cm_ref[:, gs]
        bm_g = bm_ref[:, gs]
        cb = jnp.where(causal, _dot_nt(cm_g, bm_g), 0.0)
        for e2 in range(2):
            h0 = 4 * g + 2 * e2
            ps = slice((2 * g + e2) * LANES, (2 * g + e2 + 1) * LANES)
            ss = slice(e2 * LANES, (e2 + 1) * LANES)
            x_b = xs_ref[:, ps]
            x_f = x_b.astype(F32)
            st = st_ref[g, :, ss]
            ys = []
            for h in (h0, h0 + 1):
                seg = jnp.exp2(jnp.minimum(cs2[:, h:h + 1] - u2_t[h:h + 1, :], EXP2_MAX))
                ys.append(_dot((cb * seg).astype(BF16), x_b))
            y = jnp.where(lane_lo, ys[0], ys[1])
            y = y + _dot(cm_g, st.astype(BF16)) * pair_cols(e_cs, h0)
            o_ref[:, ps] = (y + dsk_ref[0, :, ps] * x_f).astype(BF16)
            xw = (x_f * pair_cols(w_end, h0)).astype(BF16)
            dec = jnp.where(lane_lo, dec_last[:, h0:h0 + 1], dec_last[:, h0 + 1:h0 + 2])
            st_ref[g, :, ss] = st * dec + _dot_tn(bm_g, xw)


def _ssd(proj, aux, dt_bias, a_log, d_skip, layer, batch, seq):
    t = proj.shape[0]
    nc = seq // CHUNK
    row = lambda b, c: b * nc + c
    return pl.pallas_call(
        _ssd_kernel,
        grid=(batch, nc),
        in_specs=[pl.BlockSpec((CHUNK, SSD_D_INNER), lambda b, c: (row(b, c), 1)),
                  pl.BlockSpec((CHUNK, SSD_BC), lambda b, c: (row(b, c), 4)),
                  pl.BlockSpec((CHUNK, SSD_BC), lambda b, c: (row(b, c), 5)),
                  pl.BlockSpec((CHUNK, LANES), lambda b, c: (row(b, c), 0)),
                  _layer(dt_bias.shape, layer), _layer(a_log.shape, layer), _layer(d_skip.shape, layer)],
        out_specs=pl.BlockSpec((CHUNK, SSD_D_INNER), lambda b, c: (row(b, c), 0)),
        out_shape=jax.ShapeDtypeStruct((t, SSD_D_INNER), BF16),
        scratch_shapes=[pltpu.VMEM((SSD_GROUPS, SSD_STATE, 4 * SSD_HEAD_DIM), F32)],
        compiler_params=_params(("parallel", "arbitrary")),
        name="ssd",
    )(proj, proj, proj, aux, dt_bias, a_log, d_skip)


def _xattn_proj(xq_ref, kv_ref, wo_ref, d_mix):
    acc = None
    for hd in range(XA_HEADS):
        ks = slice(hd * XA_HEAD_DIM, (hd + 1) * XA_HEAD_DIM)
        vs = slice(XA_WIDTH + hd * XA_HEAD_DIM, XA_WIDTH + (hd + 1) * XA_HEAD_DIM)
        s = _dot_nt(xq_ref[:, ks], kv_ref[0, :, ks]) * (XA_HEAD_DIM ** -0.5)
        p = jnp.exp(s - jnp.max(s, axis=-1, keepdims=True))
        pv = _dot(p.astype(BF16), kv_ref[0, :, vs]) / jnp.sum(p, axis=-1, keepdims=True)
        term = _dot(pv.astype(BF16), wo_ref[0, d_mix + hd * XA_HEAD_DIM:d_mix + (hd + 1) * XA_HEAD_DIM, :])
        acc = term if acc is None else acc + term
    return acc


def _xattn_out_kernel(mix_ref, xq_ref, kv_ref, wo_ref, h_ref, o_ref):
    d_mix = mix_ref.shape[1]
    xa = _xattn_proj(xq_ref, kv_ref, wo_ref, d_mix)
    o_ref[...] = h_ref[...] + xa + _dot(mix_ref[...], wo_ref[0, 0:d_mix, :])


def _xattn_out_gated_kernel(y_ref, z_ref, nw_ref, xq_ref, kv_ref, wo_ref, h_ref, o_ref, *, tk):
    d_mix = y_ref.shape[1]
    xa = _xattn_proj(xq_ref, kv_ref, wo_ref, d_mix)
    ssq = None
    acc = None
    for c in range(d_mix // tk):
        cs = slice(c * tk, (c + 1) * tk)
        t = y_ref[:, cs].astype(F32) * _silu(z_ref[:, cs].astype(F32))
        sq = jnp.sum(t * t, axis=-1, keepdims=True)
        term = _dot((t * nw_ref[0, :, cs]).astype(BF16), wo_ref[0, cs, :])
        ssq = sq if ssq is None else ssq + sq
        acc = term if acc is None else acc + term
    o_ref[...] = h_ref[...] + xa + acc * lax.rsqrt(ssq * (1.0 / d_mix) + EPS)


def _xattn_out(mix, proj, xq_block, memkv, layer, w_out, wlayer, h, seq, gate=None, tm=512):
    t, d = h.shape
    d_mix = mix.shape[1]
    mem_len = memkv.shape[1] // (t // seq)
    per_batch = seq // tm
    specs = [pl.BlockSpec((tm, XA_WIDTH), lambda i: (i, xq_block)),
             pl.BlockSpec((1, mem_len, 2 * XA_WIDTH), lambda i: (layer, i // per_batch, 0)),
             _layer(w_out.shape, wlayer),
             pl.BlockSpec((tm, d), lambda i: (i, 0))]
    mix_spec = pl.BlockSpec((tm, d_mix), lambda i: (i, 0))
    if gate is None:
        body, in_specs, args = _xattn_out_kernel, [mix_spec] + specs, (mix, proj, memkv, w_out, h)
    else:
        z_block, norm_w = gate
        body = functools.partial(_xattn_out_gated_kernel, tk=512)
        in_specs = [mix_spec, pl.BlockSpec((tm, d_mix), lambda i: (i, z_block)),
                    _layer(norm_w.shape, wlayer)] + specs
        args = (mix, proj, norm_w, proj, memkv, w_out, h)
    return pl.pallas_call(
        body,
        grid=(t // tm,),
        in_specs=in_specs,
        out_specs=pl.BlockSpec((tm, d), lambda i: (i, 0)),
        out_shape=jax.ShapeDtypeStruct((t, d), F32),
        compiler_params=_params(("parallel",)),
        name="xattn_out",
    )(*args)


def _ffn_kernel(h_ref, nw_ref, wi_ref, wo_ref, fnw_ref, o_ref, *, hidden, th, final):
    h = h_ref[...]
    xn = _rms(h, nw_ref[0]).astype(BF16)
    acc = h
    for c in range(hidden // th):
        gate = _dot(xn, wi_ref[0, :, c * th:(c + 1) * th])
        up = _dot(xn, wi_ref[0, :, hidden + c * th:hidden + (c + 1) * th])
        act = (_silu(gate) * up).astype(BF16)
        acc = acc + _dot(act, wo_ref[0, c * th:(c + 1) * th, :])
    if final:
        acc = _rms(acc, fnw_ref[...])
    o_ref[...] = acc


def _ffn(h, norm_w, w_in, w_out, layer, final_w, final, tm=512, th=256):
    t, d = h.shape
    hidden = w_out.shape[1]
    return pl.pallas_call(
        functools.partial(_ffn_kernel, hidden=hidden, th=th, final=final),
        grid=(t // tm,),
        in_specs=[pl.BlockSpec((tm, d), lambda i: (i, 0)),
                  _layer(norm_w.shape, layer), _layer(w_in.shape, layer), _layer(w_out.shape, layer),
                  _resident((1, d))],
        out_specs=pl.BlockSpec((tm, d), lambda i: (i, 0)),
        out_shape=jax.ShapeDtypeStruct((t, d), F32),
        compiler_params=_params(("parallel",)),
        name="ffn",
    )(h, norm_w, w_in, w_out, final_w)


def _rows(v, n=None):
    v = v[:, None, :]
    return v if n is None else jnp.pad(v, ((0, 0), (0, 0), (0, n - v.shape[2])))


def _reorder_in_proj(w, n_lead, n_small):
    small = jnp.pad(w[:, :, n_lead:n_lead + n_small], ((0, 0), (0, 0), (0, LANES - n_small)))
    return jnp.concatenate([w[:, :, :n_lead], w[:, :, n_lead + n_small:], small], axis=2).astype(BF16)


def kernel(x, mem, mix_norm, mem_norm, w_mem_kv, gla_w_in, gla_w_gate2, gla_b_gate, gla_head_norm, gla_w_out, ssd_w_in, ssd_conv_w, ssd_conv_b, ssd_dt_bias, ssd_a_log, ssd_d, ssd_norm, ssd_w_out, ffn_norm, ffn_w_in, ffn_w_out, final_norm):
    batch, seq, d = x.shape
    depth = mix_norm.shape[0]
    t = batch * seq
    h = x.reshape(t, d)

    gla_lead = 2 * GLA_DK + 2 * GLA_DV
    ssd_lead = SSD_D_INNER + SSD_CONV_DIM
    gla_w = _reorder_in_proj(gla_w_in, gla_lead, GLA_GATE_RANK)
    ssd_w = _reorder_in_proj(ssd_w_in, ssd_lead, SSD_HEADS)
    gla_wg = jnp.pad(gla_w_gate2, ((0, 0), (0, LANES - GLA_GATE_RANK), (0, 0))).astype(BF16)
    gla_wo = gla_w_out.astype(BF16)
    ssd_wo = ssd_w_out.astype(BF16)
    ffn_wi = ffn_w_in.astype(BF16)
    ffn_wo = ffn_w_out.astype(BF16)
    mix_nw, ffn_nw = _rows(mix_norm), _rows(ffn_norm)
    gla_bg, gla_hn = _rows(gla_b_gate), _rows(gla_head_norm)
    ssd_cb, ssd_nw = _rows(ssd_conv_b), _rows(ssd_norm)
    ssd_dtb, ssd_alog = _rows(ssd_dt_bias, LANES), _rows(ssd_a_log, LANES)
    ssd_dsk = _rows(jnp.repeat(ssd_d, SSD_HEAD_DIM, axis=1))
    final_w = final_norm.reshape(1, d)

    memkv = _memkv(mem.reshape(-1, d), _rows(mem_norm), w_mem_kv.astype(BF16))

    for i in range(depth):
        j = i // 2
        if i % 2 == 0:
            n_main = gla_lead + XA_WIDTH
            proj, aux = _inproj(h, mix_nw, gla_w, i, j, n_main)
            mix = _gla(proj, aux, gla_wg, gla_bg, gla_hn, j, batch, seq)
            h = _xattn_out(mix, proj, n_main // XA_WIDTH - 1, memkv, i, gla_wo, j, h, seq)
        else:
            n_main = ssd_lead + XA_WIDTH
            proj, aux = _inproj(h, mix_nw, ssd_w, i, j, n_main,
                                conv=(ssd_conv_w, ssd_cb, SSD_D_INNER), seq=seq)
            y = _ssd(proj, aux, ssd_dtb, ssd_alog, ssd_dsk, j, batch, seq)
            h = _xattn_out(y, proj, n_main // XA_WIDTH - 1, memkv, i, ssd_wo, j, h, seq,
                           gate=(0, ssd_nw))
        h = _ffn(h, ffn_nw, ffn_wi, ffn_wo, i, final_w, final=(i == depth - 1))
    return h.reshape(batch, seq, d)
```

```python
import functools

import jax
import jax.numpy as jnp
from jax import lax
from jax.experimental import pallas as pl
from jax.experimental.pallas import tpu as pltpu

F32 = jnp.float32
BF16 = jnp.bfloat16
EPS = 1e-6
LOG2E = 1.4426950408889634

LANES = 128
BF16_ROWS = 16
XA_HEADS = 4
XA_HEAD_DIM = 256
XA_WIDTH = XA_HEADS * XA_HEAD_DIM
GLA_HEADS = 4
GLA_HEAD_K = 128
GLA_HEAD_V = 256
GLA_DK = GLA_HEADS * GLA_HEAD_K
GLA_DV = GLA_HEADS * GLA_HEAD_V
GLA_GATE_RANK = 16
GLA_GATE_TAU = 16.0
SSD_D_INNER = 2048
SSD_HEAD_DIM = 64
SSD_HEADS = 32
SSD_GROUPS = 8
SSD_STATE = 128
SSD_CONV_K = 4
SSD_BC = SSD_GROUPS * SSD_STATE
SSD_CONV_DIM = SSD_D_INNER + 2 * SSD_BC
CHUNK = 128
VMEM_LIMIT = 56 * 1024 * 1024

NT = (((1,), (1,)), ((), ()))
TN = (((0,), (0,)), ((), ()))


def _dot(a, b):
    return jnp.dot(a, b, preferred_element_type=F32)


def _dot_nt(a, b):
    return lax.dot_general(a, b, NT, preferred_element_type=F32)


def _dot_tn(a, b):
    return lax.dot_general(a, b, TN, preferred_element_type=F32)


def _rms(x, w):
    return x * lax.rsqrt(jnp.mean(x * x, axis=-1, keepdims=True) + EPS) * w


def _silu(x):
    hx = 0.5 * x
    return hx + hx * jnp.tanh(hx)


def _softplus(x):
    return jnp.maximum(x, 0.0) + jnp.log(1.0 + jnp.exp(-jnp.abs(x)))


def _causal(n):
    row = lax.broadcasted_iota(jnp.int32, (n, n), 0)
    col = lax.broadcasted_iota(jnp.int32, (n, n), 1)
    return row >= col


def _cumsum_rows(tril, x):
    x1 = x.astype(BF16)
    r1 = x - x1.astype(F32)
    x2 = r1.astype(BF16)
    x3 = (r1 - x2.astype(F32)).astype(BF16)
    return _dot(tril, x1) + _dot(tril, x2) + _dot(tril, x3)


def _params(sem):
    return pltpu.CompilerParams(dimension_semantics=sem, vmem_limit_bytes=VMEM_LIMIT)


def _resident(shape):
    nd = len(shape)
    return pl.BlockSpec(shape, lambda *_: (0,) * nd, pipeline_mode=pl.Buffered(1))


def _layer(shape, layer):
    nd = len(shape)
    return pl.BlockSpec((1,) + tuple(shape[1:]), lambda *_: (layer,) + (0,) * (nd - 1),
                        pipeline_mode=pl.Buffered(1))


def _inproj_kernel(x_ref, nw_ref, w_ref, main_ref, aux_ref, *, n_main, tn):
    xn = _rms(x_ref[...], nw_ref[0]).astype(BF16)
    for c in range(n_main // tn):
        cs = slice(c * tn, (c + 1) * tn)
        main_ref[:, cs] = _dot(xn, w_ref[0, :, cs]).astype(BF16)
    aux_ref[...] = _dot(xn, w_ref[0, :, n_main:])


def _inproj_conv_kernel(x_ref, xp_ref, nw_ref, w_ref, cw_ref, cb_ref, main_ref, aux_ref, *,
                        n_main, conv_lo, conv_hi, tn, blocks_per_seq):
    tm = x_ref.shape[0]
    hist = xp_ref.shape[0]
    xn = _rms(x_ref[...], nw_ref[0]).astype(BF16)
    keep = (pl.program_id(0) % blocks_per_seq != 0).astype(F32)
    xp = (_rms(xp_ref[...], nw_ref[0]) * keep).astype(BF16)
    lhs = jnp.concatenate([xp, xn], axis=0)
    sub = lax.broadcasted_iota(jnp.int32, (1, 8, tn), 1)
    is_conv = lambda c: conv_lo <= c * tn < conv_hi
    conv_chunks = [c for c in range(n_main // tn) if is_conv(c)]
    plain_chunks = [c for c in range(n_main // tn) if not is_conv(c)]
    order = []
    for k in range(max(len(conv_chunks), len(plain_chunks))):
        order += conv_chunks[k:k + 1] + plain_chunks[k:k + 1]
    for c in order:
        cs = slice(c * tn, (c + 1) * tn)
        w = w_ref[0, :, cs]
        if is_conv(c):
            cc = slice(c * tn - conv_lo, (c + 1) * tn - conv_lo)
            r = _dot(lhs, w).reshape((hist + tm) // 8, 8, tn)
            hw = 0.5 * cw_ref[0, :, cc]
            acc = 0.5 * cb_ref[0, :, cc] + r[hist // 8:] * hw[SSD_CONV_K - 1:SSD_CONV_K, :]
            for s in range(1, SSD_CONV_K):
                rr = pltpu.roll(r, s, 1)
                shifted = jnp.where(sub < s, rr[hist // 8 - 1:-1], rr[hist // 8:])
                acc = acc + shifted * hw[SSD_CONV_K - 1 - s:SSD_CONV_K - s, :]
            act = acc + acc * jnp.tanh(acc)
            main_ref[:, cs] = act.reshape(tm, tn).astype(BF16)
        else:
            main_ref[:, cs] = _dot(xn, w).astype(BF16)
    aux_ref[...] = _dot(xn, w_ref[0, :, n_main:])


def _inproj(h, norm_w, w, layer, wlayer, n_main, conv=None, seq=None, tm=512, tn=512):
    t, d = h.shape
    n = w.shape[2]
    out_specs = [pl.BlockSpec((tm, n_main), lambda i: (i, 0)),
                 pl.BlockSpec((tm, n - n_main), lambda i: (i, 0))]
    out_shape = [jax.ShapeDtypeStruct((t, n_main), BF16),
                 jax.ShapeDtypeStruct((t, n - n_main), F32)]
    x_spec = pl.BlockSpec((tm, d), lambda i: (i, 0))
    if conv is None:
        return pl.pallas_call(
            functools.partial(_inproj_kernel, n_main=n_main, tn=tn),
            grid=(t // tm,),
            in_specs=[x_spec, _layer(norm_w.shape, layer), _layer(w.shape, wlayer)],
            out_specs=out_specs, out_shape=out_shape,
            compiler_params=_params(("parallel",)),
            name="inproj",
        )(h, norm_w, w)
    conv_w, conv_b, conv_lo = conv
    per = tm // BF16_ROWS
    prev_spec = pl.BlockSpec((BF16_ROWS, d), lambda i: (jnp.maximum(i * per - 1, 0), 0))
    return pl.pallas_call(
        functools.partial(_inproj_conv_kernel, n_main=n_main, conv_lo=conv_lo,
                          conv_hi=conv_lo + conv_w.shape[2], tn=tn, blocks_per_seq=seq // tm),
        grid=(t // tm,),
        in_specs=[x_spec, prev_spec, _layer(norm_w.shape, layer), _layer(w.shape, wlayer),
                  _layer(conv_w.shape, wlayer), _layer(conv_b.shape, wlayer)],
        out_specs=out_specs, out_shape=out_shape,
        compiler_params=_params(("parallel",)),
        name="inproj_conv",
    )(h, h, norm_w, w, conv_w, conv_b)


def _memkv_kernel(mem_ref, nw_ref, w_ref, o_ref):
    xn = _rms(mem_ref[...], nw_ref[0]).astype(BF16)
    o_ref[0] = _dot(xn, w_ref[0]).astype(BF16)


def _memkv(mem2d, mem_norm, w_kv):
    depth, d, n = w_kv.shape
    rows = mem2d.shape[0]
    return pl.pallas_call(
        _memkv_kernel,
        grid=(depth,),
        in_specs=[_resident((rows, d)),
                  pl.BlockSpec((1, 1, d), lambda i: (i, 0, 0)),
                  pl.BlockSpec((1, d, n), lambda i: (i, 0, 0))],
        out_specs=pl.BlockSpec((1, rows, n), lambda i: (i, 0, 0)),
        out_shape=jax.ShapeDtypeStruct((depth, rows, n), BF16),
        compiler_params=_params(("parallel",)),
        name="memkv",
    )(mem2d, mem_norm, w_kv)


def _gla_kernel(q_ref, k_ref, v_ref, g_ref, gl_ref, wg_ref, bg_ref, hn_ref, o_ref, st_ref):
    L = CHUNK

    @pl.when(pl.program_id(1) == 0)
    def _():
        st_ref[...] = jnp.zeros_like(st_ref)

    causal = _causal(L)
    tril = jnp.where(causal, 1.0, 0.0).astype(BF16)
    for bi in range(q_ref.shape[0]):
        pre = _dot(gl_ref[bi].astype(BF16), wg_ref[0]) + bg_ref[0]
        log_a = (jnp.minimum(pre, 0.0) - jnp.log(1.0 + jnp.exp(-jnp.abs(pre)))) * (1.0 / GLA_GATE_TAU)
        b = _cumsum_rows(tril, log_a)
        b_mid = b[L // 2 - 1:L // 2, :]
        b_last = b[L - 1:L, :]
        q = q_ref[bi].astype(F32) * (GLA_HEAD_K ** -0.5) * jnp.exp(b - b_mid)
        k = k_ref[bi].astype(F32) * jnp.exp(b_mid - b)
        q_mid = q.astype(BF16)
        k_mid = k.astype(BF16)
        q_dec = (q * jnp.exp(b_mid)).astype(BF16)
        k_end = (k * jnp.exp(b_last - b_mid)).astype(BF16)
        dec = jnp.exp(b_last)
        for h in range(GLA_HEADS):
            ks = slice(h * GLA_HEAD_K, (h + 1) * GLA_HEAD_K)
            vs = slice(h * GLA_HEAD_V, (h + 1) * GLA_HEAD_V)
            s = jnp.where(causal, _dot_nt(q_mid[:, ks], k_mid[:, ks]), 0.0).astype(BF16)
            v = v_ref[bi, :, vs]
            st = st_ref[bi * GLA_HEADS + h]
            o = _dot(s, v) + _dot_nt(q_dec[:, ks], st.astype(BF16))
            st_ref[bi * GLA_HEADS + h] = st * dec[:, ks] + _dot_tn(v, k_end[:, ks])
            g = g_ref[bi, :, vs].astype(F32)
            o_ref[bi, :, vs] = (_rms(o, hn_ref[0]) * _silu(g)).astype(BF16)


def _gla(proj, aux, wg, bg, hnorm, layer, batch, seq):
    t, n = proj.shape
    nb = 4 if batch % 4 == 0 else 1
    proj3 = proj.reshape(batch, seq, n)
    aux3 = aux.reshape(batch, seq, LANES)
    out = pl.pallas_call(
        _gla_kernel,
        grid=(batch // nb, seq // CHUNK),
        in_specs=[pl.BlockSpec((nb, CHUNK, GLA_DK), lambda b, c: (b, c, 0)),
                  pl.BlockSpec((nb, CHUNK, GLA_DK), lambda b, c: (b, c, 1)),
                  pl.BlockSpec((nb, CHUNK, GLA_DV), lambda b, c: (b, c, 1)),
                  pl.BlockSpec((nb, CHUNK, GLA_DV), lambda b, c: (b, c, 2)),
                  pl.BlockSpec((nb, CHUNK, LANES), lambda b, c: (b, c, 0)),
                  _layer(wg.shape, layer), _layer(bg.shape, layer), _layer(hnorm.shape, layer)],
        out_specs=pl.BlockSpec((nb, CHUNK, GLA_DV), lambda b, c: (b, c, 0)),
        out_shape=jax.ShapeDtypeStruct((batch, seq, GLA_DV), BF16),
        scratch_shapes=[pltpu.VMEM((nb * GLA_HEADS, GLA_HEAD_V, GLA_HEAD_K), F32)],
        compiler_params=_params(("parallel", "arbitrary")),
        name="gla",
    )(proj3, proj3, proj3, proj3, aux3, wg, bg, hnorm)
    return out.reshape(t, GLA_DV)


def _ssd_kernel(xs_ref, bm_ref, cm_ref, dt_ref, dtb_ref, alog_ref, dsk_ref, ex_ref, o_ref, st_ref, dtx_ref):
    L = CHUNK
    P = SSD_HEAD_DIM
    N = SSD_STATE

    @pl.when(pl.program_id(1) == 0)
    def _():
        st_ref[...] = jnp.zeros_like(st_ref)

    causal = _causal(L)
    tril = jnp.where(causal, 1.0, 0.0).astype(BF16)
    a = -jnp.exp(alog_ref[0])
    lane_lo = lax.broadcasted_iota(jnp.int32, (1, LANES), 1) < P
    seqs = range(xs_ref.shape[0])
    cs2, cs2_t, last2 = [], [], []
    for bi in seqs:
        dt = _softplus(dt_ref[bi] + dtb_ref[0])
        cs2.append(_cumsum_rows(tril, dt * a) * LOG2E)
        cs2_t.append(cs2[bi].T)
        last2.append(cs2[bi][L - 1:L, :])
        dt_hi = dt.astype(BF16)
        dt_lo = (dt - dt_hi.astype(F32)).astype(BF16)
        dtx_ref[bi] = _dot(jnp.concatenate([dt_hi, dt_lo], axis=1), ex_ref[...])

    for g in range(SSD_GROUPS):
        gs = slice(g * N, (g + 1) * N)
        cb = [jnp.where(causal, _dot_nt(cm_ref[bi, :, gs], bm_ref[bi, :, gs]), 0.0) for bi in seqs]
        for e2 in range(2):
            h0 = 4 * g + 2 * e2
            ps = slice((2 * g + e2) * LANES, (2 * g + e2 + 1) * LANES)
            ss = slice(e2 * LANES, (e2 + 1) * LANES)
            for bi in seqs:
                x_f = xs_ref[bi, :, ps].astype(F32)
                xd = x_f * dtx_ref[bi, :, ps]
                xd_b = xd.astype(BF16)
                st = st_ref[bi * SSD_GROUPS + g, :, ss]
                cols, ys = [], []
                for h in (h0, h0 + 1):
                    col = jnp.broadcast_to(cs2[bi][:, h:h + 1], (L, LANES))
                    seg = jnp.exp2(jnp.minimum(col - cs2_t[bi][h:h + 1, :], 0.0))
                    cols.append(col)
                    ys.append(_dot((cb[bi] * seg).astype(BF16), xd_b))
                col = jnp.where(lane_lo, cols[0], cols[1])
                last = jnp.where(lane_lo, last2[bi][:, h0:h0 + 1], last2[bi][:, h0 + 1:h0 + 2])
                y = (jnp.where(lane_lo, ys[0], ys[1])
                     + _dot(cm_ref[bi, :, gs], st.astype(BF16)) * jnp.exp2(col))
                o_ref[bi, :, ps] = (y + dsk_ref[0, :, ps] * x_f).astype(BF16)
                xw = (xd * jnp.exp2(last - col)).astype(BF16)
                st_ref[bi * SSD_GROUPS + g, :, ss] = st * jnp.exp2(last) + _dot_tn(bm_ref[bi, :, gs], xw)


def _ssd(proj, aux, dt_bias, a_log, d_skip, layer, batch, seq):
    t, n = proj.shape
    nb = 2 if batch % 2 == 0 else 1
    proj3 = proj.reshape(batch, seq, n)
    aux3 = aux.reshape(batch, seq, LANES)
    head_of = jnp.arange(SSD_D_INNER, dtype=jnp.int32) // SSD_HEAD_DIM
    expand = (jnp.arange(2 * LANES, dtype=jnp.int32)[:, None] % LANES == head_of[None, :]).astype(BF16)
    out = pl.pallas_call(
        _ssd_kernel,
        grid=(batch // nb, seq // CHUNK),
        in_specs=[pl.BlockSpec((nb, CHUNK, SSD_D_INNER), lambda b, c: (b, c, 1)),
                  pl.BlockSpec((nb, CHUNK, SSD_BC), lambda b, c: (b, c, 4)),
                  pl.BlockSpec((nb, CHUNK, SSD_BC), lambda b, c: (b, c, 5)),
                  pl.BlockSpec((nb, CHUNK, LANES), lambda b, c: (b, c, 0)),
                  _layer(dt_bias.shape, layer), _layer(a_log.shape, layer), _layer(d_skip.shape, layer),
                  _resident(expand.shape)],
        out_specs=pl.BlockSpec((nb, CHUNK, SSD_D_INNER), lambda b, c: (b, c, 0)),
        out_shape=jax.ShapeDtypeStruct((batch, seq, SSD_D_INNER), BF16),
        scratch_shapes=[pltpu.VMEM((nb * SSD_GROUPS, SSD_STATE, 4 * SSD_HEAD_DIM), F32),
                        pltpu.VMEM((nb, CHUNK, SSD_D_INNER), F32)],
        compiler_params=_params(("parallel", "arbitrary")),
        name="ssd",
    )(proj3, proj3, proj3, aux3, dt_bias, a_log, d_skip, expand)
    return out.reshape(t, SSD_D_INNER)


def _xattn_proj(xq_ref, kv_ref, wo_ref, d_mix):
    acc = None
    for hd in range(XA_HEADS):
        ks = slice(hd * XA_HEAD_DIM, (hd + 1) * XA_HEAD_DIM)
        vs = slice(XA_WIDTH + hd * XA_HEAD_DIM, XA_WIDTH + (hd + 1) * XA_HEAD_DIM)
        s = _dot_nt(xq_ref[:, ks], kv_ref[0, :, ks]) * (XA_HEAD_DIM ** -0.5)
        p = jnp.exp(s - jnp.max(s, axis=-1, keepdims=True))
        pv = _dot(p.astype(BF16), kv_ref[0, :, vs]) / jnp.sum(p, axis=-1, keepdims=True)
        term = _dot(pv.astype(BF16), wo_ref[0, d_mix + hd * XA_HEAD_DIM:d_mix + (hd + 1) * XA_HEAD_DIM, :])
        acc = term if acc is None else acc + term
    return acc


def _xattn_out_kernel(mix_ref, xq_ref, kv_ref, wo_ref, h_ref, o_ref):
    d_mix = mix_ref.shape[1]
    xa = _xattn_proj(xq_ref, kv_ref, wo_ref, d_mix)
    o_ref[...] = h_ref[...] + xa + _dot(mix_ref[...], wo_ref[0, 0:d_mix, :])


def _xattn_out_gated_kernel(y_ref, z_ref, nw_ref, xq_ref, kv_ref, wo_ref, h_ref, o_ref, *, tk):
    d_mix = y_ref.shape[1]
    xa = _xattn_proj(xq_ref, kv_ref, wo_ref, d_mix)
    ssq = None
    acc = None
    for c in range(d_mix // tk):
        cs = slice(c * tk, (c + 1) * tk)
        t = y_ref[:, cs].astype(F32) * _silu(z_ref[:, cs].astype(F32))
        sq = jnp.sum(t * t, axis=-1, keepdims=True)
        term = _dot((t * nw_ref[0, :, cs]).astype(BF16), wo_ref[0, cs, :])
        ssq = sq if ssq is None else ssq + sq
        acc = term if acc is None else acc + term
    o_ref[...] = h_ref[...] + xa + acc * lax.rsqrt(ssq * (1.0 / d_mix) + EPS)


def _xattn_out(mix, proj, xq_block, memkv, layer, w_out, wlayer, h, seq, gate=None, tm=512):
    t, d = h.shape
    d_mix = mix.shape[1]
    mem_len = memkv.shape[1] // (t // seq)
    per_batch = seq // tm
    specs = [pl.BlockSpec((tm, XA_WIDTH), lambda i: (i, xq_block)),
             pl.BlockSpec((1, mem_len, 2 * XA_WIDTH), lambda i: (layer, i // per_batch, 0)),
             _layer(w_out.shape, wlayer),
             pl.BlockSpec((tm, d), lambda i: (i, 0))]
    mix_spec = pl.BlockSpec((tm, d_mix), lambda i: (i, 0))
    if gate is None:
        body, in_specs, args = _xattn_out_kernel, [mix_spec] + specs, (mix, proj, memkv, w_out, h)
    else:
        z_block, norm_w = gate
        body = functools.partial(_xattn_out_gated_kernel, tk=512)
        in_specs = [mix_spec, pl.BlockSpec((tm, d_mix), lambda i: (i, z_block)),
                    _layer(norm_w.shape, wlayer)] + specs
        args = (mix, proj, norm_w, proj, memkv, w_out, h)
    return pl.pallas_call(
        body,
        grid=(t // tm,),
        in_specs=in_specs,
        out_specs=pl.BlockSpec((tm, d), lambda i: (i, 0)),
        out_shape=jax.ShapeDtypeStruct((t, d), F32),
        compiler_params=_params(("parallel",)),
        name="xattn_out",
    )(*args)


def _ffn_kernel(h_ref, nw_ref, wi_ref, wo_ref, fnw_ref, o_ref, *, hidden, th, final):
    h = h_ref[...]
    xn = _rms(h, nw_ref[0]).astype(BF16)
    acc = h
    for c in range(hidden // th):
        gate = _dot(xn, wi_ref[0, :, c * th:(c + 1) * th])
        up = _dot(xn, wi_ref[0, :, hidden + c * th:hidden + (c + 1) * th])
        act = (_silu(gate) * up).astype(BF16)
        acc = acc + _dot(act, wo_ref[0, c * th:(c + 1) * th, :])
    if final:
        acc = _rms(acc, fnw_ref[...])
    o_ref[...] = acc


def _ffn(h, norm_w, w_in, w_out, layer, final_w, final, tm=512, th=256):
    t, d = h.shape
    hidden = w_out.shape[1]
    return pl.pallas_call(
        functools.partial(_ffn_kernel, hidden=hidden, th=th, final=final),
        grid=(t // tm,),
        in_specs=[pl.BlockSpec((tm, d), lambda i: (i, 0)),
                  _layer(norm_w.shape, layer), _layer(w_in.shape, layer), _layer(w_out.shape, layer),
                  _resident((1, d))],
        out_specs=pl.BlockSpec((tm, d), lambda i: (i, 0)),
        out_shape=jax.ShapeDtypeStruct((t, d), F32),
        compiler_params=_params(("parallel",)),
        name="ffn",
    )(h, norm_w, w_in, w_out, final_w)


def _rows(v, n=None):
    v = v[:, None, :]
    return v if n is None else jnp.pad(v, ((0, 0), (0, 0), (0, n - v.shape[2])))


def _reorder_in_proj(w, n_lead, n_small):
    small = jnp.pad(w[:, :, n_lead:n_lead + n_small], ((0, 0), (0, 0), (0, LANES - n_small)))
    return jnp.concatenate([w[:, :, :n_lead], w[:, :, n_lead + n_small:], small], axis=2).astype(BF16)


def kernel(x, mem, mix_norm, mem_norm, w_mem_kv, gla_w_in, gla_w_gate2, gla_b_gate, gla_head_norm, gla_w_out, ssd_w_in, ssd_conv_w, ssd_conv_b, ssd_dt_bias, ssd_a_log, ssd_d, ssd_norm, ssd_w_out, ffn_norm, ffn_w_in, ffn_w_out, final_norm):
    batch, seq, d = x.shape
    depth = mix_norm.shape[0]
    t = batch * seq
    h = x.reshape(t, d)

    gla_lead = 2 * GLA_DK + 2 * GLA_DV
    ssd_lead = SSD_D_INNER + SSD_CONV_DIM
    gla_w = _reorder_in_proj(gla_w_in, gla_lead, GLA_GATE_RANK)
    ssd_w = _reorder_in_proj(ssd_w_in, ssd_lead, SSD_HEADS)
    gla_wg = jnp.pad(gla_w_gate2, ((0, 0), (0, LANES - GLA_GATE_RANK), (0, 0))).astype(BF16)
    gla_wo = gla_w_out.astype(BF16)
    ssd_wo = ssd_w_out.astype(BF16)
    ffn_wi = ffn_w_in.astype(BF16)
    ffn_wo = ffn_w_out.astype(BF16)
    mix_nw, ffn_nw = _rows(mix_norm), _rows(ffn_norm)
    gla_bg, gla_hn = _rows(gla_b_gate), _rows(gla_head_norm)
    ssd_cb, ssd_nw = _rows(ssd_conv_b), _rows(ssd_norm)
    ssd_dtb, ssd_alog = _rows(ssd_dt_bias, LANES), _rows(ssd_a_log, LANES)
    ssd_dsk = _rows(jnp.repeat(ssd_d, SSD_HEAD_DIM, axis=1))
    final_w = final_norm.reshape(1, d)

    memkv = _memkv(mem.reshape(-1, d), _rows(mem_norm), w_mem_kv.astype(BF16))

    for i in range(depth):
        j = i // 2
        if i % 2 == 0:
            proj, aux = _inproj(h, mix_nw, gla_w, i, j, gla_lead + XA_WIDTH)
            mix = _gla(proj, aux, gla_wg, gla_bg, gla_hn, j, batch, seq)
            h = _xattn_out(mix, proj, gla_lead // XA_WIDTH, memkv, i, gla_wo, j, h, seq)
        else:
            proj, aux = _inproj(h, mix_nw, ssd_w, i, j, ssd_lead + XA_WIDTH,
                                conv=(ssd_conv_w, ssd_cb, SSD_D_INNER), seq=seq)
            y = _ssd(proj, aux, ssd_dtb, ssd_alog, ssd_dsk, j, batch, seq)
            h = _xattn_out(y, proj, ssd_lead // XA_WIDTH, memkv, i, ssd_wo, j, h, seq,
                           gate=(0, ssd_nw))
        h = _ffn(h, ffn_nw, ffn_wi, ffn_wo, i, final_w, final=(i == depth - 1))
    return h.reshape(batch, seq, d)
```

```python
import functools

import jax
import jax.numpy as jnp
from jax import lax
from jax.experimental import pallas as pl
from jax.experimental.pallas import tpu as pltpu

F32 = jnp.float32
BF16 = jnp.bfloat16
EPS = 1e-6
LOG2E = 1.4426950408889634

LANES = 128
BF16_ROWS = 16
XA_HEADS = 4
XA_HEAD_DIM = 256
XA_WIDTH = XA_HEADS * XA_HEAD_DIM
GLA_HEADS = 4
GLA_HEAD_K = 128
GLA_HEAD_V = 256
GLA_DK = GLA_HEADS * GLA_HEAD_K
GLA_DV = GLA_HEADS * GLA_HEAD_V
GLA_GATE_RANK = 16
GLA_GATE_TAU = 16.0
SSD_D_INNER = 2048
SSD_HEAD_DIM = 64
SSD_HEADS = 32
SSD_GROUPS = 8
SSD_STATE = 128
SSD_CONV_K = 4
SSD_BC = SSD_GROUPS * SSD_STATE
SSD_CONV_DIM = SSD_D_INNER + 2 * SSD_BC
CHUNK = 128
VMEM_LIMIT = 56 * 1024 * 1024

NT = (((1,), (1,)), ((), ()))
TN = (((0,), (0,)), ((), ()))


def _dot(a, b):
    return jnp.dot(a, b, preferred_element_type=F32)


def _dot_nt(a, b):
    return lax.dot_general(a, b, NT, preferred_element_type=F32)


def _dot_tn(a, b):
    return lax.dot_general(a, b, TN, preferred_element_type=F32)


def _rms(x, w):
    return x * lax.rsqrt(jnp.mean(x * x, axis=-1, keepdims=True) + EPS) * w


def _silu(x):
    hx = 0.5 * x
    return hx + hx * jnp.tanh(hx)


def _softplus(x):
    return jnp.maximum(x, 0.0) + jnp.log(1.0 + jnp.exp(-jnp.abs(x)))


def _causal(n):
    row = lax.broadcasted_iota(jnp.int32, (n, n), 0)
    col = lax.broadcasted_iota(jnp.int32, (n, n), 1)
    return row >= col


def _cumsum_rows(tril, x):
    x1 = x.astype(BF16)
    r1 = x - x1.astype(F32)
    x2 = r1.astype(BF16)
    x3 = (r1 - x2.astype(F32)).astype(BF16)
    return _dot(tril, x1) + _dot(tril, x2) + _dot(tril, x3)


def _params(sem):
    return pltpu.CompilerParams(dimension_semantics=sem, vmem_limit_bytes=VMEM_LIMIT)


def _resident(shape):
    nd = len(shape)
    return pl.BlockSpec(shape, lambda *_: (0,) * nd, pipeline_mode=pl.Buffered(1))


def _layer(shape, layer):
    nd = len(shape)
    return pl.BlockSpec((1,) + tuple(shape[1:]), lambda *_: (layer,) + (0,) * (nd - 1),
                        pipeline_mode=pl.Buffered(1))


def _inproj_kernel(x_ref, nw_ref, w_ref, main_ref, aux_ref, *, n_main, tn):
    xn = _rms(x_ref[...], nw_ref[0]).astype(BF16)
    for c in range(n_main // tn):
        cs = slice(c * tn, (c + 1) * tn)
        main_ref[:, cs] = _dot(xn, w_ref[0, :, cs]).astype(BF16)
    aux_ref[...] = _dot(xn, w_ref[0, :, n_main:])


def _inproj_conv_kernel(x_ref, xp_ref, nw_ref, w_ref, cw_ref, cb_ref, main_ref, aux_ref, *,
                        n_main, conv_lo, conv_hi, tn, blocks_per_seq):
    tm = x_ref.shape[0]
    hist = xp_ref.shape[0]
    xn = _rms(x_ref[...], nw_ref[0]).astype(BF16)
    keep = (pl.program_id(0) % blocks_per_seq != 0).astype(F32)
    xp = (_rms(xp_ref[...], nw_ref[0]) * keep).astype(BF16)
    lhs = jnp.concatenate([xp, xn], axis=0)
    sub = lax.broadcasted_iota(jnp.int32, (1, 8, tn), 1)
    is_conv = lambda c: conv_lo <= c * tn < conv_hi
    conv_chunks = [c for c in range(n_main // tn) if is_conv(c)]
    plain_chunks = [c for c in range(n_main // tn) if not is_conv(c)]
    order = []
    for k in range(max(len(conv_chunks), len(plain_chunks))):
        order += conv_chunks[k:k + 1] + plain_chunks[k:k + 1]
    for c in order:
        cs = slice(c * tn, (c + 1) * tn)
        w = w_ref[0, :, cs]
        if is_conv(c):
            cc = slice(c * tn - conv_lo, (c + 1) * tn - conv_lo)
            r = _dot(lhs, w).reshape((hist + tm) // 8, 8, tn)
            hw = 0.5 * cw_ref[0, :, cc]
            acc = 0.5 * cb_ref[0, :, cc] + r[hist // 8:] * hw[SSD_CONV_K - 1:SSD_CONV_K, :]
            for s in range(1, SSD_CONV_K):
                rr = pltpu.roll(r, s, 1)
                shifted = jnp.where(sub < s, rr[hist // 8 - 1:-1], rr[hist // 8:])
                acc = acc + shifted * hw[SSD_CONV_K - 1 - s:SSD_CONV_K - s, :]
            act = acc + acc * jnp.tanh(acc)
            main_ref[:, cs] = act.reshape(tm, tn).astype(BF16)
        else:
            main_ref[:, cs] = _dot(xn, w).astype(BF16)
    aux_ref[...] = _dot(xn, w_ref[0, :, n_main:])


def _inproj(h, norm_w, w, layer, wlayer, n_main, conv=None, seq=None, tm=512, tn=512):
    t, d = h.shape
    n = w.shape[2]
    out_specs = [pl.BlockSpec((tm, n_main), lambda i: (i, 0)),
                 pl.BlockSpec((tm, n - n_main), lambda i: (i, 0))]
    out_shape = [jax.ShapeDtypeStruct((t, n_main), BF16),
                 jax.ShapeDtypeStruct((t, n - n_main), F32)]
    x_spec = pl.BlockSpec((tm, d), lambda i: (i, 0))
    if conv is None:
        return pl.pallas_call(
            functools.partial(_inproj_kernel, n_main=n_main, tn=tn),
            grid=(t // tm,),
            in_specs=[x_spec, _layer(norm_w.shape, layer), _layer(w.shape, wlayer)],
            out_specs=out_specs, out_shape=out_shape,
            compiler_params=_params(("parallel",)),
            name="inproj",
        )(h, norm_w, w)
    conv_w, conv_b, conv_lo = conv
    per = tm // BF16_ROWS
    prev_spec = pl.BlockSpec((BF16_ROWS, d), lambda i: (jnp.maximum(i * per - 1, 0), 0))
    return pl.pallas_call(
        functools.partial(_inproj_conv_kernel, n_main=n_main, conv_lo=conv_lo,
                          conv_hi=conv_lo + conv_w.shape[2], tn=tn, blocks_per_seq=seq // tm),
        grid=(t // tm,),
        in_specs=[x_spec, prev_spec, _layer(norm_w.shape, layer), _layer(w.shape, wlayer),
                  _layer(conv_w.shape, wlayer), _layer(conv_b.shape, wlayer)],
        out_specs=out_specs, out_shape=out_shape,
        compiler_params=_params(("parallel",)),
        name="inproj_conv",
    )(h, h, norm_w, w, conv_w, conv_b)


def _memkv_kernel(mem_ref, nw_ref, w_ref, o_ref):
    xn = _rms(mem_ref[...], nw_ref[0]).astype(BF16)
    o_ref[0] = _dot(xn, w_ref[0]).astype(BF16)


def _memkv(mem2d, mem_norm, w_kv):
    depth, d, n = w_kv.shape
    rows = mem2d.shape[0]
    return pl.pallas_call(
        _memkv_kernel,
        grid=(depth,),
        in_specs=[_resident((rows, d)),
                  pl.BlockSpec((1, 1, d), lambda i: (i, 0, 0)),
                  pl.BlockSpec((1, d, n), lambda i: (i, 0, 0))],
        out_specs=pl.BlockSpec((1, rows, n), lambda i: (i, 0, 0)),
        out_shape=jax.ShapeDtypeStruct((depth, rows, n), BF16),
        compiler_params=_params(("parallel",)),
        name="memkv",
    )(mem2d, mem_norm, w_kv)


def _gla_kernel(q_ref, k_ref, v_ref, g_ref, gl_ref, wg_ref, bg_ref, hn_ref, o_ref, st_ref):
    L = CHUNK

    @pl.when(pl.program_id(1) == 0)
    def _():
        st_ref[...] = jnp.zeros_like(st_ref)

    causal = _causal(L)
    tril = jnp.where(causal, 1.0, 0.0).astype(BF16)
    for bi in range(q_ref.shape[0]):
        pre = _dot(gl_ref[bi].astype(BF16), wg_ref[0]) + bg_ref[0]
        log2_a = (jnp.minimum(pre, 0.0) * (LOG2E / GLA_GATE_TAU)
                  - jnp.log2(1.0 + jnp.exp2(jnp.abs(pre) * -LOG2E)) * (1.0 / GLA_GATE_TAU))
        b = _cumsum_rows(tril, log2_a)
        b_mid = b[L // 2 - 1:L // 2, :]
        b_last = b[L - 1:L, :]
        q = q_ref[bi].astype(F32) * (GLA_HEAD_K ** -0.5) * jnp.exp2(b - b_mid)
        k = k_ref[bi].astype(F32) * jnp.exp2(b_mid - b)
        q_mid = q.astype(BF16)
        k_mid = k.astype(BF16)
        q_dec = (q * jnp.exp2(b_mid)).astype(BF16)
        k_end = (k * jnp.exp2(b_last - b_mid)).astype(BF16)
        dec = jnp.exp2(b_last)
        for h in range(GLA_HEADS):
            ks = slice(h * GLA_HEAD_K, (h + 1) * GLA_HEAD_K)
            vs = slice(h * GLA_HEAD_V, (h + 1) * GLA_HEAD_V)
            s = jnp.where(causal, _dot_nt(q_mid[:, ks], k_mid[:, ks]), 0.0).astype(BF16)
            v = v_ref[bi, :, vs]
            st = st_ref[bi * GLA_HEADS + h]
            o = _dot(s, v) + _dot_nt(q_dec[:, ks], st.astype(BF16))
            st_ref[bi * GLA_HEADS + h] = st * dec[:, ks] + _dot_tn(v, k_end[:, ks])
            g = g_ref[bi, :, vs].astype(F32)
            o_ref[bi, :, vs] = (_rms(o, hn_ref[0]) * _silu(g)).astype(BF16)


def _gla(proj, aux, wg, bg, hnorm, layer, batch, seq):
    t, n = proj.shape
    nb = 4 if batch % 4 == 0 else 1
    proj3 = proj.reshape(batch, seq, n)
    aux3 = aux.reshape(batch, seq, LANES)
    out = pl.pallas_call(
        _gla_kernel,
        grid=(batch // nb, seq // CHUNK),
        in_specs=[pl.BlockSpec((nb, CHUNK, GLA_DK), lambda b, c: (b, c, 0)),
                  pl.BlockSpec((nb, CHUNK, GLA_DK), lambda b, c: (b, c, 1)),
                  pl.BlockSpec((nb, CHUNK, GLA_DV), lambda b, c: (b, c, 1)),
                  pl.BlockSpec((nb, CHUNK, GLA_DV), lambda b, c: (b, c, 2)),
                  pl.BlockSpec((nb, CHUNK, LANES), lambda b, c: (b, c, 0)),
                  _layer(wg.shape, layer), _layer(bg.shape, layer), _layer(hnorm.shape, layer)],
        out_specs=pl.BlockSpec((nb, CHUNK, GLA_DV), lambda b, c: (b, c, 0)),
        out_shape=jax.ShapeDtypeStruct((batch, seq, GLA_DV), BF16),
        scratch_shapes=[pltpu.VMEM((nb * GLA_HEADS, GLA_HEAD_V, GLA_HEAD_K), F32)],
        compiler_params=_params(("parallel", "arbitrary")),
        name="gla",
    )(proj3, proj3, proj3, proj3, aux3, wg, bg, hnorm)
    return out.reshape(t, GLA_DV)


def _ssd_kernel(xs_ref, bm_ref, cm_ref, dt_ref, dtb_ref, alog_ref, dsk_ref, ex_ref, o_ref, st_ref, dtx_ref):
    L = CHUNK
    P = SSD_HEAD_DIM
    N = SSD_STATE

    @pl.when(pl.program_id(1) == 0)
    def _():
        st_ref[...] = jnp.zeros_like(st_ref)

    causal = _causal(L)
    tril = jnp.where(causal, 1.0, 0.0).astype(BF16)
    a = -jnp.exp(alog_ref[0])
    lane_lo = lax.broadcasted_iota(jnp.int32, (1, LANES), 1) < P
    seqs = range(xs_ref.shape[0])
    cs2, cs2_t, last2 = [], [], []
    for bi in seqs:
        dt = _softplus(dt_ref[bi] + dtb_ref[0])
        cs2.append(_cumsum_rows(tril, dt * a) * LOG2E)
        cs2_t.append(cs2[bi].T)
        last2.append(cs2[bi][L - 1:L, :])
        dt_hi = dt.astype(BF16)
        dt_lo = (dt - dt_hi.astype(F32)).astype(BF16)
        dtx_ref[bi] = _dot(jnp.concatenate([dt_hi, dt_lo], axis=1), ex_ref[...])

    for g in range(SSD_GROUPS):
        gs = slice(g * N, (g + 1) * N)
        cb = [jnp.where(causal, _dot_nt(cm_ref[bi, :, gs], bm_ref[bi, :, gs]), 0.0) for bi in seqs]
        for e2 in range(2):
            h0 = 4 * g + 2 * e2
            ps = slice((2 * g + e2) * LANES, (2 * g + e2 + 1) * LANES)
            ss = slice(e2 * LANES, (e2 + 1) * LANES)
            for bi in seqs:
                x_f = xs_ref[bi, :, ps].astype(F32)
                xd = x_f * dtx_ref[bi, :, ps]
                xd_b = xd.astype(BF16)
                st = st_ref[bi * SSD_GROUPS + g, :, ss]
                cols, ys = [], []
                for h in (h0, h0 + 1):
                    col = jnp.broadcast_to(cs2[bi][:, h:h + 1], (L, LANES))
                    seg = jnp.exp2(jnp.minimum(col - cs2_t[bi][h:h + 1, :], 0.0))
                    cols.append(col)
                    ys.append(_dot((cb[bi] * seg).astype(BF16), xd_b))
                col = jnp.where(lane_lo, cols[0], cols[1])
                last = jnp.where(lane_lo, last2[bi][:, h0:h0 + 1], last2[bi][:, h0 + 1:h0 + 2])
                y = (jnp.where(lane_lo, ys[0], ys[1])
                     + _dot(cm_ref[bi, :, gs], st.astype(BF16)) * jnp.exp2(col))
                o_ref[bi, :, ps] = (y + dsk_ref[0, :, ps] * x_f).astype(BF16)
                xw = (xd * jnp.exp2(last - col)).astype(BF16)
                st_ref[bi * SSD_GROUPS + g, :, ss] = st * jnp.exp2(last) + _dot_tn(bm_ref[bi, :, gs], xw)


def _ssd(proj, aux, dt_bias, a_log, d_skip, layer, batch, seq):
    t, n = proj.shape
    nb = 4 if batch % 4 == 0 else 1
    proj3 = proj.reshape(batch, seq, n)
    aux3 = aux.reshape(batch, seq, LANES)
    head_of = jnp.arange(SSD_D_INNER, dtype=jnp.int32) // SSD_HEAD_DIM
    expand = (jnp.arange(2 * LANES, dtype=jnp.int32)[:, None] % LANES == head_of[None, :]).astype(BF16)
    out = pl.pallas_call(
        _ssd_kernel,
        grid=(batch // nb, seq // CHUNK),
        in_specs=[pl.BlockSpec((nb, CHUNK, SSD_D_INNER), lambda b, c: (b, c, 1)),
                  pl.BlockSpec((nb, CHUNK, SSD_BC), lambda b, c: (b, c, 4)),
                  pl.BlockSpec((nb, CHUNK, SSD_BC), lambda b, c: (b, c, 5)),
                  pl.BlockSpec((nb, CHUNK, LANES), lambda b, c: (b, c, 0)),
                  _layer(dt_bias.shape, layer), _layer(a_log.shape, layer), _layer(d_skip.shape, layer),
                  _resident(expand.shape)],
        out_specs=pl.BlockSpec((nb, CHUNK, SSD_D_INNER), lambda b, c: (b, c, 0)),
        out_shape=jax.ShapeDtypeStruct((batch, seq, SSD_D_INNER), BF16),
        scratch_shapes=[pltpu.VMEM((nb * SSD_GROUPS, SSD_STATE, 4 * SSD_HEAD_DIM), F32),
                        pltpu.VMEM((nb, CHUNK, SSD_D_INNER), F32)],
        compiler_params=_params(("parallel", "arbitrary")),
        name="ssd",
    )(proj3, proj3, proj3, aux3, dt_bias, a_log, d_skip, expand)
    return out.reshape(t, SSD_D_INNER)


def _xattn_proj(xq_ref, kv_ref, wo_ref, d_mix):
    acc = None
    for hd in range(XA_HEADS):
        ks = slice(hd * XA_HEAD_DIM, (hd + 1) * XA_HEAD_DIM)
        vs = slice(XA_WIDTH + hd * XA_HEAD_DIM, XA_WIDTH + (hd + 1) * XA_HEAD_DIM)
        s = _dot_nt(xq_ref[:, ks], kv_ref[0, :, ks]) * (XA_HEAD_DIM ** -0.5)
        p = jnp.exp(s - jnp.max(s, axis=-1, keepdims=True))
        pv = _dot(p.astype(BF16), kv_ref[0, :, vs]) / jnp.sum(p, axis=-1, keepdims=True)
        term = _dot(pv.astype(BF16), wo_ref[0, d_mix + hd * XA_HEAD_DIM:d_mix + (hd + 1) * XA_HEAD_DIM, :])
        acc = term if acc is None else acc + term
    return acc


def _xattn_out_kernel(mix_ref, xq_ref, kv_ref, wo_ref, h_ref, o_ref):
    d_mix = mix_ref.shape[1]
    xa = _xattn_proj(xq_ref, kv_ref, wo_ref, d_mix)
    o_ref[...] = h_ref[...] + xa + _dot(mix_ref[...], wo_ref[0, 0:d_mix, :])


def _xattn_out_gated_kernel(y_ref, z_ref, nw_ref, xq_ref, kv_ref, wo_ref, h_ref, o_ref, *, tk):
    d_mix = y_ref.shape[1]
    xa = _xattn_proj(xq_ref, kv_ref, wo_ref, d_mix)
    ssq = None
    acc = None
    for c in range(d_mix // tk):
        cs = slice(c * tk, (c + 1) * tk)
        t = y_ref[:, cs].astype(F32) * _silu(z_ref[:, cs].astype(F32))
        sq = jnp.sum(t * t, axis=-1, keepdims=True)
        term = _dot((t * nw_ref[0, :, cs]).astype(BF16), wo_ref[0, cs, :])
        ssq = sq if ssq is None else ssq + sq
        acc = term if acc is None else acc + term
    o_ref[...] = h_ref[...] + xa + acc * lax.rsqrt(ssq * (1.0 / d_mix) + EPS)


def _xattn_out(mix, proj, xq_block, memkv, layer, w_out, wlayer, h, seq, gate=None):
    t, d = h.shape
    tm = 1024 if gate is None else 512
    d_mix = mix.shape[1]
    mem_len = memkv.shape[1] // (t // seq)
    per_batch = seq // tm
    specs = [pl.BlockSpec((tm, XA_WIDTH), lambda i: (i, xq_block)),
             pl.BlockSpec((1, mem_len, 2 * XA_WIDTH), lambda i: (layer, i // per_batch, 0)),
             _layer(w_out.shape, wlayer),
             pl.BlockSpec((tm, d), lambda i: (i, 0))]
    mix_spec = pl.BlockSpec((tm, d_mix), lambda i: (i, 0))
    if gate is None:
        body, in_specs, args = _xattn_out_kernel, [mix_spec] + specs, (mix, proj, memkv, w_out, h)
    else:
        z_block, norm_w = gate
        body = functools.partial(_xattn_out_gated_kernel, tk=512)
        in_specs = [mix_spec, pl.BlockSpec((tm, d_mix), lambda i: (i, z_block)),
                    _layer(norm_w.shape, wlayer)] + specs
        args = (mix, proj, norm_w, proj, memkv, w_out, h)
    return pl.pallas_call(
        body,
        grid=(t // tm,),
        in_specs=in_specs,
        out_specs=pl.BlockSpec((tm, d), lambda i: (i, 0)),
        out_shape=jax.ShapeDtypeStruct((t, d), F32),
        compiler_params=_params(("parallel",)),
        name="xattn_out",
    )(*args)


def _ffn_kernel(h_ref, nw_ref, wi_ref, wo_ref, fnw_ref, o_ref, *, hidden, th, final):
    h = h_ref[...]
    xn = _rms(h, nw_ref[0]).astype(BF16)
    acc = h
    for c in range(hidden // th):
        gate = _dot(xn, wi_ref[0, :, c * th:(c + 1) * th])
        up = _dot(xn, wi_ref[0, :, hidden + c * th:hidden + (c + 1) * th])
        act = (_silu(gate) * up).astype(BF16)
        acc = acc + _dot(act, wo_ref[0, c * th:(c + 1) * th, :])
    if final:
        acc = _rms(acc, fnw_ref[...])
    o_ref[...] = acc


def _ffn(h, norm_w, w_in, w_out, layer, final_w, final, tm=512, th=256):
    t, d = h.shape
    hidden = w_out.shape[1]
    return pl.pallas_call(
        functools.partial(_ffn_kernel, hidden=hidden, th=th, final=final),
        grid=(t // tm,),
        in_specs=[pl.BlockSpec((tm, d), lambda i: (i, 0)),
                  _layer(norm_w.shape, layer), _layer(w_in.shape, layer), _layer(w_out.shape, layer),
                  _resident((1, d))],
        out_specs=pl.BlockSpec((tm, d), lambda i: (i, 0)),
        out_shape=jax.ShapeDtypeStruct((t, d), F32),
        compiler_params=_params(("parallel",)),
        name="ffn",
    )(h, norm_w, w_in, w_out, final_w)


def _rows(v, n=None):
    v = v[:, None, :]
    return v if n is None else jnp.pad(v, ((0, 0), (0, 0), (0, n - v.shape[2])))


def _reorder_kernel(w_ref, o_ref, *, n_lead, n_small):
    w = w_ref[0]
    n_xq = w.shape[1] - n_lead - n_small
    o_ref[0, :, 0:n_lead] = w[:, 0:n_lead].astype(BF16)
    o_ref[0, :, n_lead:n_lead + n_xq] = w[:, n_lead + n_small:].astype(BF16)
    pad = jnp.zeros((w.shape[0], LANES - n_small), F32)
    o_ref[0, :, n_lead + n_xq:] = jnp.concatenate([w[:, n_lead:n_lead + n_small], pad], axis=1).astype(BF16)


def _reorder_in_proj(w, n_lead, n_small, rows=256):
    layers, d, n = w.shape
    n_out = n - n_small + LANES
    return pl.pallas_call(
        functools.partial(_reorder_kernel, n_lead=n_lead, n_small=n_small),
        grid=(layers, d // rows),
        in_specs=[pl.BlockSpec((1, rows, n), lambda l, r: (l, r, 0))],
        out_specs=pl.BlockSpec((1, rows, n_out), lambda l, r: (l, r, 0)),
        out_shape=jax.ShapeDtypeStruct((layers, d, n_out), BF16),
        compiler_params=_params(("parallel", "parallel")),
        name="reorder_in_proj",
    )(w)


def kernel(x, mem, mix_norm, mem_norm, w_mem_kv, gla_w_in, gla_w_gate2, gla_b_gate, gla_head_norm, gla_w_out, ssd_w_in, ssd_conv_w, ssd_conv_b, ssd_dt_bias, ssd_a_log, ssd_d, ssd_norm, ssd_w_out, ffn_norm, ffn_w_in, ffn_w_out, final_norm):
    batch, seq, d = x.shape
    depth = mix_norm.shape[0]
    t = batch * seq
    h = x.reshape(t, d)

    gla_lead = 2 * GLA_DK + 2 * GLA_DV
    ssd_lead = SSD_D_INNER + SSD_CONV_DIM
    gla_w = _reorder_in_proj(gla_w_in, gla_lead, GLA_GATE_RANK)
    ssd_w = _reorder_in_proj(ssd_w_in, ssd_lead, SSD_HEADS)
    gla_wg = jnp.pad(gla_w_gate2, ((0, 0), (0, LANES - GLA_GATE_RANK), (0, 0))).astype(BF16)
    gla_wo = gla_w_out.astype(BF16)
    ssd_wo = ssd_w_out.astype(BF16)
    ffn_wi = ffn_w_in.astype(BF16)
    ffn_wo = ffn_w_out.astype(BF16)
    mix_nw, ffn_nw = _rows(mix_norm), _rows(ffn_norm)
    gla_bg, gla_hn = _rows(gla_b_gate), _rows(gla_head_norm)
    ssd_cb, ssd_nw = _rows(ssd_conv_b), _rows(ssd_norm)
    ssd_dtb, ssd_alog = _rows(ssd_dt_bias, LANES), _rows(ssd_a_log, LANES)
    ssd_dsk = _rows(jnp.repeat(ssd_d, SSD_HEAD_DIM, axis=1))
    final_w = final_norm.reshape(1, d)

    memkv = _memkv(mem.reshape(-1, d), _rows(mem_norm), w_mem_kv.astype(BF16))

    for i in range(depth):
        j = i // 2
        if i % 2 == 0:
            proj, aux = _inproj(h, mix_nw, gla_w, i, j, gla_lead + XA_WIDTH)
            mix = _gla(proj, aux, gla_wg, gla_bg, gla_hn, j, batch, seq)
            h = _xattn_out(mix, proj, gla_lead // XA_WIDTH, memkv, i, gla_wo, j, h, seq)
        else:
            proj, aux = _inproj(h, mix_nw, ssd_w, i, j, ssd_lead + XA_WIDTH,
                                conv=(ssd_conv_w, ssd_cb, SSD_D_INNER), seq=seq)
            y = _ssd(proj, aux, ssd_dtb, ssd_alog, ssd_dsk, j, batch, seq)
            h = _xattn_out(y, proj, ssd_lead // XA_WIDTH, memkv, i, ssd_wo, j, h, seq,
                           gate=(0, ssd_nw))
        h = _ffn(h, ffn_nw, ffn_wi, ffn_wo, i, final_w, final=(i == depth - 1))
    return h.reshape(batch, seq, d)
```

```python
import functools

import jax
import jax.numpy as jnp
from jax import lax
from jax.experimental import pallas as pl
from jax.experimental.pallas import tpu as pltpu

F32 = jnp.float32
BF16 = jnp.bfloat16
EPS = 1e-6
LOG2E = 1.4426950408889634

LANES = 128
BF16_ROWS = 16
XA_HEADS = 4
XA_HEAD_DIM = 256
XA_WIDTH = XA_HEADS * XA_HEAD_DIM
GLA_HEADS = 4
GLA_HEAD_K = 128
GLA_HEAD_V = 256
GLA_DK = GLA_HEADS * GLA_HEAD_K
GLA_DV = GLA_HEADS * GLA_HEAD_V
GLA_GATE_RANK = 16
GLA_GATE_TAU = 16.0
SSD_D_INNER = 2048
SSD_HEAD_DIM = 64
SSD_HEADS = 32
SSD_GROUPS = 8
SSD_STATE = 128
SSD_CONV_K = 4
SSD_BC = SSD_GROUPS * SSD_STATE
SSD_CONV_DIM = SSD_D_INNER + 2 * SSD_BC
CHUNK = 128
VMEM_LIMIT = 56 * 1024 * 1024

NT = (((1,), (1,)), ((), ()))
TN = (((0,), (0,)), ((), ()))


def _dot(a, b):
    return jnp.dot(a, b, preferred_element_type=F32)


def _dot_nt(a, b):
    return lax.dot_general(a, b, NT, preferred_element_type=F32)


def _dot_tn(a, b):
    return lax.dot_general(a, b, TN, preferred_element_type=F32)


def _rms(x, w):
    return x * lax.rsqrt(jnp.mean(x * x, axis=-1, keepdims=True) + EPS) * w


def _silu(x):
    hx = 0.5 * x
    return hx + hx * jnp.tanh(hx)


def _softplus(x):
    return jnp.maximum(x, 0.0) + jnp.log(1.0 + jnp.exp(-jnp.abs(x)))


def _causal(n):
    row = lax.broadcasted_iota(jnp.int32, (n, n), 0)
    col = lax.broadcasted_iota(jnp.int32, (n, n), 1)
    return row >= col


def _cumsum_rows(tril, x):
    x1 = x.astype(BF16)
    r1 = x - x1.astype(F32)
    x2 = r1.astype(BF16)
    x3 = (r1 - x2.astype(F32)).astype(BF16)
    return _dot(tril, x1) + _dot(tril, x2) + _dot(tril, x3)


def _params(sem):
    return pltpu.CompilerParams(dimension_semantics=sem, vmem_limit_bytes=VMEM_LIMIT)


def _resident(shape):
    nd = len(shape)
    return pl.BlockSpec(shape, lambda *_: (0,) * nd, pipeline_mode=pl.Buffered(1))


def _layer(shape, layer):
    nd = len(shape)
    return pl.BlockSpec((1,) + tuple(shape[1:]), lambda *_: (layer,) + (0,) * (nd - 1),
                        pipeline_mode=pl.Buffered(1))


def _inproj_kernel(x_ref, nw_ref, w_ref, main_ref, aux_ref, *, n_main, tn):
    xn = _rms(x_ref[...], nw_ref[0]).astype(BF16)
    for c in range(n_main // tn):
        cs = slice(c * tn, (c + 1) * tn)
        main_ref[:, cs] = _dot(xn, w_ref[0, :, cs]).astype(BF16)
    aux_ref[...] = _dot(xn, w_ref[0, :, n_main:])


def _inproj_conv_kernel(x_ref, xp_ref, nw_ref, w_ref, cw_ref, cb_ref, main_ref, aux_ref, *,
                        n_main, conv_lo, conv_hi, tn, blocks_per_seq):
    tm = x_ref.shape[0]
    hist = xp_ref.shape[0]
    xn = _rms(x_ref[...], nw_ref[0]).astype(BF16)
    keep = (pl.program_id(0) % blocks_per_seq != 0).astype(F32)
    xp = (_rms(xp_ref[...], nw_ref[0]) * keep).astype(BF16)
    lhs = jnp.concatenate([xp, xn], axis=0)
    sub = lax.broadcasted_iota(jnp.int32, (1, 8, tn), 1)
    is_conv = lambda c: conv_lo <= c * tn < conv_hi
    conv_chunks = [c for c in range(n_main // tn) if is_conv(c)]
    plain_chunks = [c for c in range(n_main // tn) if not is_conv(c)]
    order = []
    for k in range(max(len(conv_chunks), len(plain_chunks))):
        order += conv_chunks[k:k + 1] + plain_chunks[k:k + 1]
    for c in order:
        cs = slice(c * tn, (c + 1) * tn)
        w = w_ref[0, :, cs]
        if is_conv(c):
            cc = slice(c * tn - conv_lo, (c + 1) * tn - conv_lo)
            r = _dot(lhs, w).reshape((hist + tm) // 8, 8, tn)
            hw = 0.5 * cw_ref[0, :, cc]
            acc = 0.5 * cb_ref[0, :, cc] + r[hist // 8:] * hw[SSD_CONV_K - 1:SSD_CONV_K, :]
            for s in range(1, SSD_CONV_K):
                rr = pltpu.roll(r, s, 1)
                shifted = jnp.where(sub < s, rr[hist // 8 - 1:-1], rr[hist // 8:])
                acc = acc + shifted * hw[SSD_CONV_K - 1 - s:SSD_CONV_K - s, :]
            act = acc + acc * jnp.tanh(acc)
            main_ref[:, cs] = act.reshape(tm, tn).astype(BF16)
        else:
            main_ref[:, cs] = _dot(xn, w).astype(BF16)
    aux_ref[...] = _dot(xn, w_ref[0, :, n_main:])


def _inproj(h, norm_w, w, layer, wlayer, n_main, conv=None, seq=None, tm=512, tn=512):
    t, d = h.shape
    n = w.shape[2]
    out_specs = [pl.BlockSpec((tm, n_main), lambda i: (i, 0)),
                 pl.BlockSpec((tm, n - n_main), lambda i: (i, 0))]
    out_shape = [jax.ShapeDtypeStruct((t, n_main), BF16),
                 jax.ShapeDtypeStruct((t, n - n_main), F32)]
    x_spec = pl.BlockSpec((tm, d), lambda i: (i, 0))
    if conv is None:
        return pl.pallas_call(
            functools.partial(_inproj_kernel, n_main=n_main, tn=tn),
            grid=(t // tm,),
            in_specs=[x_spec, _layer(norm_w.shape, layer), _layer(w.shape, wlayer)],
            out_specs=out_specs, out_shape=out_shape,
            compiler_params=_params(("parallel",)),
            name="inproj",
        )(h, norm_w, w)
    conv_w, conv_b, conv_lo = conv
    per = tm // BF16_ROWS
    prev_spec = pl.BlockSpec((BF16_ROWS, d), lambda i: (jnp.maximum(i * per - 1, 0), 0))
    return pl.pallas_call(
        functools.partial(_inproj_conv_kernel, n_main=n_main, conv_lo=conv_lo,
                          conv_hi=conv_lo + conv_w.shape[2], tn=tn, blocks_per_seq=seq // tm),
        grid=(t // tm,),
        in_specs=[x_spec, prev_spec, _layer(norm_w.shape, layer), _layer(w.shape, wlayer),
                  _layer(conv_w.shape, wlayer), _layer(conv_b.shape, wlayer)],
        out_specs=out_specs, out_shape=out_shape,
        compiler_params=_params(("parallel",)),
        name="inproj_conv",
    )(h, h, norm_w, w, conv_w, conv_b)


def _memkv_kernel(mem_ref, nw_ref, w_ref, o_ref):
    xn = _rms(mem_ref[...], nw_ref[0]).astype(BF16)
    o_ref[0] = _dot(xn, w_ref[0].astype(BF16)).astype(BF16)


def _memkv(mem2d, mem_norm, w_kv):
    depth, d, n = w_kv.shape
    rows = mem2d.shape[0]
    return pl.pallas_call(
        _memkv_kernel,
        grid=(depth,),
        in_specs=[_resident((rows, d)),
                  pl.BlockSpec((1, 1, d), lambda i: (i, 0, 0)),
                  pl.BlockSpec((1, d, n), lambda i: (i, 0, 0))],
        out_specs=pl.BlockSpec((1, rows, n), lambda i: (i, 0, 0)),
        out_shape=jax.ShapeDtypeStruct((depth, rows, n), BF16),
        compiler_params=_params(("parallel",)),
        name="memkv",
    )(mem2d, mem_norm, w_kv)


def _gla_kernel(q_ref, k_ref, v_ref, g_ref, gl_ref, wg_ref, bg_ref, hn_ref, o_ref, st_ref):
    L = CHUNK

    @pl.when(pl.program_id(1) == 0)
    def _():
        st_ref[...] = jnp.zeros_like(st_ref)

    causal = _causal(L)
    tril = jnp.where(causal, 1.0, 0.0).astype(BF16)
    for bi in range(q_ref.shape[0]):
        pre = _dot(gl_ref[bi].astype(BF16), wg_ref[0]) + bg_ref[0]
        log2_a = (jnp.minimum(pre, 0.0) * (LOG2E / GLA_GATE_TAU)
                  - jnp.log2(1.0 + jnp.exp2(jnp.abs(pre) * -LOG2E)) * (1.0 / GLA_GATE_TAU))
        b = _cumsum_rows(tril, log2_a)
        b_mid = b[L // 2 - 1:L // 2, :]
        b_last = b[L - 1:L, :]
        q = q_ref[bi].astype(F32) * (GLA_HEAD_K ** -0.5) * jnp.exp2(b - b_mid)
        k = k_ref[bi].astype(F32) * jnp.exp2(b_mid - b)
        q_mid = q.astype(BF16)
        k_mid = k.astype(BF16)
        q_dec = (q * jnp.exp2(b_mid)).astype(BF16)
        k_end = (k * jnp.exp2(b_last - b_mid)).astype(BF16)
        dec = jnp.exp2(b_last)
        for h in range(GLA_HEADS):
            ks = slice(h * GLA_HEAD_K, (h + 1) * GLA_HEAD_K)
            vs = slice(h * GLA_HEAD_V, (h + 1) * GLA_HEAD_V)
            s = jnp.where(causal, _dot_nt(q_mid[:, ks], k_mid[:, ks]), 0.0).astype(BF16)
            v = v_ref[bi, :, vs]
            st = st_ref[bi * GLA_HEADS + h]
            o = _dot(s, v) + _dot_nt(q_dec[:, ks], st.astype(BF16))
            st_ref[bi * GLA_HEADS + h] = st * dec[:, ks] + _dot_tn(v, k_end[:, ks])
            g = g_ref[bi, :, vs].astype(F32)
            o_ref[bi, :, vs] = (_rms(o, hn_ref[0]) * _silu(g)).astype(BF16)


def _gla(proj, aux, wg, bg, hnorm, layer, batch, seq):
    t, n = proj.shape
    nb = 4 if batch % 4 == 0 else 1
    proj3 = proj.reshape(batch, seq, n)
    aux3 = aux.reshape(batch, seq, LANES)
    out = pl.pallas_call(
        _gla_kernel,
        grid=(batch // nb, seq // CHUNK),
        in_specs=[pl.BlockSpec((nb, CHUNK, GLA_DK), lambda b, c: (b, c, 0)),
                  pl.BlockSpec((nb, CHUNK, GLA_DK), lambda b, c: (b, c, 1)),
                  pl.BlockSpec((nb, CHUNK, GLA_DV), lambda b, c: (b, c, 1)),
                  pl.BlockSpec((nb, CHUNK, GLA_DV), lambda b, c: (b, c, 2)),
                  pl.BlockSpec((nb, CHUNK, LANES), lambda b, c: (b, c, 0)),
                  _layer(wg.shape, layer), _layer(bg.shape, layer), _layer(hnorm.shape, layer)],
        out_specs=pl.BlockSpec((nb, CHUNK, GLA_DV), lambda b, c: (b, c, 0)),
        out_shape=jax.ShapeDtypeStruct((batch, seq, GLA_DV), BF16),
        scratch_shapes=[pltpu.VMEM((nb * GLA_HEADS, GLA_HEAD_V, GLA_HEAD_K), F32)],
        compiler_params=_params(("parallel", "arbitrary")),
        name="gla",
    )(proj3, proj3, proj3, proj3, aux3, wg, bg, hnorm)
    return out.reshape(t, GLA_DV)


def _ssd_kernel(xs_ref, bm_ref, cm_ref, dt_ref, dtb_ref, alog_ref, dsk_ref, ex_ref, o_ref, st_ref, dtx_ref):
    L = CHUNK
    P = SSD_HEAD_DIM
    N = SSD_STATE

    @pl.when(pl.program_id(1) == 0)
    def _():
        st_ref[...] = jnp.zeros_like(st_ref)

    causal = _causal(L)
    tril = jnp.where(causal, 1.0, 0.0).astype(BF16)
    a = -jnp.exp(alog_ref[0])
    lane_lo = lax.broadcasted_iota(jnp.int32, (1, LANES), 1) < P
    seqs = range(xs_ref.shape[0])
    cs2, cs2_t, last2 = [], [], []
    for bi in seqs:
        dt = _softplus(dt_ref[bi] + dtb_ref[0])
        cs2.append(_cumsum_rows(tril, dt * a) * LOG2E)
        cs2_t.append(cs2[bi].T)
        last2.append(cs2[bi][L - 1:L, :])
        dt_hi = dt.astype(BF16)
        dt_lo = (dt - dt_hi.astype(F32)).astype(BF16)
        dtx_ref[bi] = _dot(jnp.concatenate([dt_hi, dt_lo], axis=1), ex_ref[...])

    for g in range(SSD_GROUPS):
        gs = slice(g * N, (g + 1) * N)
        cb = [jnp.where(causal, _dot_nt(cm_ref[bi, :, gs], bm_ref[bi, :, gs]), 0.0) for bi in seqs]
        for e2 in range(2):
            h0 = 4 * g + 2 * e2
            ps = slice((2 * g + e2) * LANES, (2 * g + e2 + 1) * LANES)
            ss = slice(e2 * LANES, (e2 + 1) * LANES)
            for bi in seqs:
                x_f = xs_ref[bi, :, ps].astype(F32)
                xd = x_f * dtx_ref[bi, :, ps]
                xd_b = xd.astype(BF16)
                st = st_ref[bi * SSD_GROUPS + g, :, ss]
                cols, ys = [], []
                for h in (h0, h0 + 1):
                    col = jnp.broadcast_to(cs2[bi][:, h:h + 1], (L, LANES))
                    seg = jnp.exp2(jnp.minimum(col - cs2_t[bi][h:h + 1, :], 0.0))
                    cols.append(col)
                    ys.append(_dot((cb[bi] * seg).astype(BF16), xd_b))
                col = jnp.where(lane_lo, cols[0], cols[1])
                last = jnp.where(lane_lo, last2[bi][:, h0:h0 + 1], last2[bi][:, h0 + 1:h0 + 2])
                y = (jnp.where(lane_lo, ys[0], ys[1])
                     + _dot(cm_ref[bi, :, gs], st.astype(BF16)) * jnp.exp2(col))
                o_ref[bi, :, ps] = (y + dsk_ref[0, :, ps] * x_f).astype(BF16)
                xw = (xd * jnp.exp2(last - col)).astype(BF16)
                st_ref[bi * SSD_GROUPS + g, :, ss] = st * jnp.exp2(last) + _dot_tn(bm_ref[bi, :, gs], xw)


def _ssd(proj, aux, dt_bias, a_log, d_skip, layer, batch, seq):
    t, n = proj.shape
    nb = 4 if batch % 4 == 0 else 1
    proj3 = proj.reshape(batch, seq, n)
    aux3 = aux.reshape(batch, seq, LANES)
    head_of = jnp.arange(SSD_D_INNER, dtype=jnp.int32) // SSD_HEAD_DIM
    expand = (jnp.arange(2 * LANES, dtype=jnp.int32)[:, None] % LANES == head_of[None, :]).astype(BF16)
    out = pl.pallas_call(
        _ssd_kernel,
        grid=(batch // nb, seq // CHUNK),
        in_specs=[pl.BlockSpec((nb, CHUNK, SSD_D_INNER), lambda b, c: (b, c, 1)),
                  pl.BlockSpec((nb, CHUNK, SSD_BC), lambda b, c: (b, c, 4)),
                  pl.BlockSpec((nb, CHUNK, SSD_BC), lambda b, c: (b, c, 5)),
                  pl.BlockSpec((nb, CHUNK, LANES), lambda b, c: (b, c, 0)),
                  _layer(dt_bias.shape, layer), _layer(a_log.shape, layer), _layer(d_skip.shape, layer),
                  _resident(expand.shape)],
        out_specs=pl.BlockSpec((nb, CHUNK, SSD_D_INNER), lambda b, c: (b, c, 0)),
        out_shape=jax.ShapeDtypeStruct((batch, seq, SSD_D_INNER), BF16),
        scratch_shapes=[pltpu.VMEM((nb * SSD_GROUPS, SSD_STATE, 4 * SSD_HEAD_DIM), F32),
                        pltpu.VMEM((nb, CHUNK, SSD_D_INNER), F32)],
        compiler_params=_params(("parallel", "arbitrary")),
        name="ssd",
    )(proj3, proj3, proj3, aux3, dt_bias, a_log, d_skip, expand)
    return out.reshape(t, SSD_D_INNER)


def _xattn_proj(xq_ref, kv_ref, wo_ref, d_mix):
    acc = None
    for hd in range(XA_HEADS):
        ks = slice(hd * XA_HEAD_DIM, (hd + 1) * XA_HEAD_DIM)
        vs = slice(XA_WIDTH + hd * XA_HEAD_DIM, XA_WIDTH + (hd + 1) * XA_HEAD_DIM)
        s = _dot_nt(xq_ref[:, ks], kv_ref[0, :, ks]) * (XA_HEAD_DIM ** -0.5)
        p = jnp.exp(s - jnp.max(s, axis=-1, keepdims=True))
        pv = _dot(p.astype(BF16), kv_ref[0, :, vs]) / jnp.sum(p, axis=-1, keepdims=True)
        term = _dot(pv.astype(BF16), wo_ref[0, d_mix + hd * XA_HEAD_DIM:d_mix + (hd + 1) * XA_HEAD_DIM, :])
        acc = term if acc is None else acc + term
    return acc


def _xattn_out_kernel(mix_ref, xq_ref, kv_ref, wo_ref, h_ref, o_ref):
    d_mix = mix_ref.shape[1]
    xa = _xattn_proj(xq_ref, kv_ref, wo_ref, d_mix)
    o_ref[...] = h_ref[...] + xa + _dot(mix_ref[...], wo_ref[0, 0:d_mix, :])


def _xattn_out_gated_kernel(y_ref, z_ref, nw_ref, xq_ref, kv_ref, wo_ref, h_ref, o_ref, *, tk):
    d_mix = y_ref.shape[1]
    xa = _xattn_proj(xq_ref, kv_ref, wo_ref, d_mix)
    ssq = None
    acc = None
    for c in range(d_mix // tk):
        cs = slice(c * tk, (c + 1) * tk)
        t = y_ref[:, cs].astype(F32) * _silu(z_ref[:, cs].astype(F32))
        sq = jnp.sum(t * t, axis=-1, keepdims=True)
        term = _dot((t * nw_ref[0, :, cs]).astype(BF16), wo_ref[0, cs, :])
        ssq = sq if ssq is None else ssq + sq
        acc = term if acc is None else acc + term
    o_ref[...] = h_ref[...] + xa + acc * lax.rsqrt(ssq * (1.0 / d_mix) + EPS)


def _xattn_out(mix, proj, xq_block, memkv, layer, w_out, wlayer, h, seq, gate=None):
    t, d = h.shape
    tm = 1024 if gate is None and seq % 1024 == 0 else 512
    d_mix = mix.shape[1]
    mem_len = memkv.shape[1] // (t // seq)
    per_batch = seq // tm
    specs = [pl.BlockSpec((tm, XA_WIDTH), lambda i: (i, xq_block)),
             pl.BlockSpec((1, mem_len, 2 * XA_WIDTH), lambda i: (layer, i // per_batch, 0)),
             _layer(w_out.shape, wlayer),
             pl.BlockSpec((tm, d), lambda i: (i, 0))]
    mix_spec = pl.BlockSpec((tm, d_mix), lambda i: (i, 0))
    if gate is None:
        body, in_specs, args = _xattn_out_kernel, [mix_spec] + specs, (mix, proj, memkv, w_out, h)
    else:
        z_block, norm_w = gate
        body = functools.partial(_xattn_out_gated_kernel, tk=512)
        in_specs = [mix_spec, pl.BlockSpec((tm, d_mix), lambda i: (i, z_block)),
                    _layer(norm_w.shape, wlayer)] + specs
        args = (mix, proj, norm_w, proj, memkv, w_out, h)
    return pl.pallas_call(
        body,
        grid=(t // tm,),
        in_specs=in_specs,
        out_specs=pl.BlockSpec((tm, d), lambda i: (i, 0)),
        out_shape=jax.ShapeDtypeStruct((t, d), F32),
        compiler_params=_params(("parallel",)),
        name="xattn_out",
    )(*args)


def _ffn_kernel(h_ref, nw_ref, wi_ref, wo_ref, fnw_ref, o_ref, *, hidden, th, final):
    h = h_ref[...]
    xn = _rms(h, nw_ref[0]).astype(BF16)
    acc = h
    for c in range(hidden // th):
        gate = _dot(xn, wi_ref[0, :, c * th:(c + 1) * th])
        up = _dot(xn, wi_ref[0, :, hidden + c * th:hidden + (c + 1) * th])
        act = (_silu(gate) * up).astype(BF16)
        acc = acc + _dot(act, wo_ref[0, c * th:(c + 1) * th, :])
    if final:
        acc = _rms(acc, fnw_ref[...])
    o_ref[...] = acc


def _ffn(h, norm_w, w_in, w_out, layer, final_w, final, tm=512, th=256):
    t, d = h.shape
    hidden = w_out.shape[1]
    return pl.pallas_call(
        functools.partial(_ffn_kernel, hidden=hidden, th=th, final=final),
        grid=(t // tm,),
        in_specs=[pl.BlockSpec((tm, d), lambda i: (i, 0)),
                  _layer(norm_w.shape, layer), _layer(w_in.shape, layer), _layer(w_out.shape, layer),
                  _resident((1, d))],
        out_specs=pl.BlockSpec((tm, d), lambda i: (i, 0)),
        out_shape=jax.ShapeDtypeStruct((t, d), F32),
        compiler_params=_params(("parallel",)),
        name="ffn",
    )(h, norm_w, w_in, w_out, final_w)


def _rows(v, n=None):
    v = v[:, None, :]
    return v if n is None else jnp.pad(v, ((0, 0), (0, 0), (0, n - v.shape[2])))


def _reorder_in_proj(w, n_lead, n_small):
    small = jnp.pad(w[:, :, n_lead:n_lead + n_small], ((0, 0), (0, 0), (0, LANES - n_small)))
    return jnp.concatenate([w[:, :, :n_lead], w[:, :, n_lead + n_small:], small], axis=2).astype(BF16)


def kernel(x, mem, mix_norm, mem_norm, w_mem_kv, gla_w_in, gla_w_gate2, gla_b_gate, gla_head_norm, gla_w_out, ssd_w_in, ssd_conv_w, ssd_conv_b, ssd_dt_bias, ssd_a_log, ssd_d, ssd_norm, ssd_w_out, ffn_norm, ffn_w_in, ffn_w_out, final_norm):
    batch, seq, d = x.shape
    depth = mix_norm.shape[0]
    t = batch * seq
    h = x.reshape(t, d)

    gla_lead = 2 * GLA_DK + 2 * GLA_DV
    ssd_lead = SSD_D_INNER + SSD_CONV_DIM
    gla_w = _reorder_in_proj(gla_w_in, gla_lead, GLA_GATE_RANK)
    ssd_w = _reorder_in_proj(ssd_w_in, ssd_lead, SSD_HEADS)
    gla_wg = jnp.pad(gla_w_gate2, ((0, 0), (0, LANES - GLA_GATE_RANK), (0, 0))).astype(BF16)
    gla_wo = gla_w_out.astype(BF16)
    ssd_wo = ssd_w_out.astype(BF16)
    ffn_wi = ffn_w_in.astype(BF16)
    ffn_wo = ffn_w_out.astype(BF16)
    mix_nw, ffn_nw = _rows(mix_norm), _rows(ffn_norm)
    gla_bg, gla_hn = _rows(gla_b_gate), _rows(gla_head_norm)
    ssd_cb, ssd_nw = _rows(ssd_conv_b), _rows(ssd_norm)
    ssd_dtb, ssd_alog = _rows(ssd_dt_bias, LANES), _rows(ssd_a_log, LANES)
    ssd_dsk = _rows(jnp.repeat(ssd_d, SSD_HEAD_DIM, axis=1))
    final_w = final_norm.reshape(1, d)

    memkv = _memkv(mem.reshape(-1, d), _rows(mem_norm), w_mem_kv)

    for i in range(depth):
        j = i // 2
        if i % 2 == 0:
            proj, aux = _inproj(h, mix_nw, gla_w, i, j, gla_lead + XA_WIDTH)
            mix = _gla(proj, aux, gla_wg, gla_bg, gla_hn, j, batch, seq)
            h = _xattn_out(mix, proj, gla_lead // XA_WIDTH, memkv, i, gla_wo, j, h, seq)
        else:
            proj, aux = _inproj(h, mix_nw, ssd_w, i, j, ssd_lead + XA_WIDTH,
                                conv=(ssd_conv_w, ssd_cb, SSD_D_INNER), seq=seq)
            y = _ssd(proj, aux, ssd_dtb, ssd_alog, ssd_dsk, j, batch, seq)
            h = _xattn_out(y, proj, ssd_lead // XA_WIDTH, memkv, i, ssd_wo, j, h, seq,
                           gate=(0, ssd_nw))
        h = _ffn(h, ffn_nw, ffn_wi, ffn_wo, i, final_w, final=(i == depth - 1))
    return h.reshape(batch, seq, d)
```

```python
import functools

import jax
import jax.numpy as jnp
from jax import lax
from jax.experimental import pallas as pl
from jax.experimental.pallas import tpu as pltpu

F32 = jnp.float32
BF16 = jnp.bfloat16
EPS = 1e-6
LOG2E = 1.4426950408889634

LANES = 128
BF16_ROWS = 16
XA_HEADS = 4
XA_HEAD_DIM = 256
XA_WIDTH = XA_HEADS * XA_HEAD_DIM
GLA_HEADS = 4
GLA_HEAD_K = 128
GLA_HEAD_V = 256
GLA_DK = GLA_HEADS * GLA_HEAD_K
GLA_DV = GLA_HEADS * GLA_HEAD_V
GLA_GATE_RANK = 16
GLA_GATE_TAU = 16.0
SSD_D_INNER = 2048
SSD_HEAD_DIM = 64
SSD_HEADS = 32
SSD_GROUPS = 8
SSD_STATE = 128
SSD_CONV_K = 4
SSD_BC = SSD_GROUPS * SSD_STATE
SSD_CONV_DIM = SSD_D_INNER + 2 * SSD_BC
CHUNK = 128
VMEM_LIMIT = 56 * 1024 * 1024

NT = (((1,), (1,)), ((), ()))
TN = (((0,), (0,)), ((), ()))


def _dot(a, b):
    return jnp.dot(a, b, preferred_element_type=F32)


def _dot_nt(a, b):
    return lax.dot_general(a, b, NT, preferred_element_type=F32)


def _dot_tn(a, b):
    return lax.dot_general(a, b, TN, preferred_element_type=F32)


def _rms(x, w):
    return x * lax.rsqrt(jnp.mean(x * x, axis=-1, keepdims=True) + EPS) * w


def _silu(x):
    hx = 0.5 * x
    return hx + hx * jnp.tanh(hx)


def _softplus(x):
    return jnp.maximum(x, 0.0) + jnp.log(1.0 + jnp.exp(-jnp.abs(x)))


def _causal(n):
    row = lax.broadcasted_iota(jnp.int32, (n, n), 0)
    col = lax.broadcasted_iota(jnp.int32, (n, n), 1)
    return row >= col


def _cumsum_rows(tril, x):
    x1 = x.astype(BF16)
    r1 = x - x1.astype(F32)
    x2 = r1.astype(BF16)
    x3 = (r1 - x2.astype(F32)).astype(BF16)
    return _dot(tril, x1) + _dot(tril, x2) + _dot(tril, x3)


def _params(sem):
    return pltpu.CompilerParams(dimension_semantics=sem, vmem_limit_bytes=VMEM_LIMIT)


def _resident(shape):
    nd = len(shape)
    return pl.BlockSpec(shape, lambda *_: (0,) * nd, pipeline_mode=pl.Buffered(1))


def _layer(shape, layer):
    nd = len(shape)
    return pl.BlockSpec((1,) + tuple(shape[1:]), lambda *_: (layer,) + (0,) * (nd - 1),
                        pipeline_mode=pl.Buffered(1))


def _inproj_kernel(x_ref, nw_ref, w_ref, main_ref, aux_ref, *, n_main, tn):
    xn = _rms(x_ref[...], nw_ref[0]).astype(BF16)
    for c in range(n_main // tn):
        cs = slice(c * tn, (c + 1) * tn)
        main_ref[:, cs] = _dot(xn, w_ref[0, :, cs]).astype(BF16)
    aux_ref[...] = _dot(xn, w_ref[0, :, n_main:])


def _inproj_conv_kernel(x_ref, xp_ref, nw_ref, w_ref, cw_ref, cb_ref, main_ref, aux_ref, *,
                        n_main, conv_lo, conv_hi, tn, blocks_per_seq):
    tm = x_ref.shape[0]
    hist = xp_ref.shape[0]
    xn = _rms(x_ref[...], nw_ref[0]).astype(BF16)
    keep = (pl.program_id(0) % blocks_per_seq != 0).astype(F32)
    xp = (_rms(xp_ref[...], nw_ref[0]) * keep).astype(BF16)
    lhs = jnp.concatenate([xp, xn], axis=0)
    sub = lax.broadcasted_iota(jnp.int32, (1, 8, tn), 1)
    is_conv = lambda c: conv_lo <= c * tn < conv_hi
    conv_chunks = [c for c in range(n_main // tn) if is_conv(c)]
    plain_chunks = [c for c in range(n_main // tn) if not is_conv(c)]
    order = []
    for k in range(max(len(conv_chunks), len(plain_chunks))):
        order += conv_chunks[k:k + 1] + plain_chunks[k:k + 1]
    for c in order:
        cs = slice(c * tn, (c + 1) * tn)
        w = w_ref[0, :, cs]
        if is_conv(c):
            cc = slice(c * tn - conv_lo, (c + 1) * tn - conv_lo)
            r = _dot(lhs, w).reshape((hist + tm) // 8, 8, tn)
            hw = 0.5 * cw_ref[0, :, cc]
            acc = 0.5 * cb_ref[0, :, cc] + r[hist // 8:] * hw[SSD_CONV_K - 1:SSD_CONV_K, :]
            for s in range(1, SSD_CONV_K):
                rr = pltpu.roll(r, s, 1)
                shifted = jnp.where(sub < s, rr[hist // 8 - 1:-1], rr[hist // 8:])
                acc = acc + shifted * hw[SSD_CONV_K - 1 - s:SSD_CONV_K - s, :]
            act = acc + acc * jnp.tanh(acc)
            main_ref[:, cs] = act.reshape(tm, tn).astype(BF16)
        else:
            main_ref[:, cs] = _dot(xn, w).astype(BF16)
    aux_ref[...] = _dot(xn, w_ref[0, :, n_main:])


def _inproj(h, norm_w, w, layer, wlayer, n_main, conv=None, seq=None, tm=512, tn=512):
    t, d = h.shape
    n = w.shape[2]
    out_specs = [pl.BlockSpec((tm, n_main), lambda i: (i, 0)),
                 pl.BlockSpec((tm, n - n_main), lambda i: (i, 0))]
    out_shape = [jax.ShapeDtypeStruct((t, n_main), BF16),
                 jax.ShapeDtypeStruct((t, n - n_main), F32)]
    x_spec = pl.BlockSpec((tm, d), lambda i: (i, 0))
    if conv is None:
        return pl.pallas_call(
            functools.partial(_inproj_kernel, n_main=n_main, tn=tn),
            grid=(t // tm,),
            in_specs=[x_spec, _layer(norm_w.shape, layer), _layer(w.shape, wlayer)],
            out_specs=out_specs, out_shape=out_shape,
            compiler_params=_params(("parallel",)),
            name="inproj",
        )(h, norm_w, w)
    conv_w, conv_b, conv_lo = conv
    per = tm // BF16_ROWS
    prev_spec = pl.BlockSpec((BF16_ROWS, d), lambda i: (jnp.maximum(i * per - 1, 0), 0))
    return pl.pallas_call(
        functools.partial(_inproj_conv_kernel, n_main=n_main, conv_lo=conv_lo,
                          conv_hi=conv_lo + conv_w.shape[2], tn=tn, blocks_per_seq=seq // tm),
        grid=(t // tm,),
        in_specs=[x_spec, prev_spec, _layer(norm_w.shape, layer), _layer(w.shape, wlayer),
                  _layer(conv_w.shape, wlayer), _layer(conv_b.shape, wlayer)],
        out_specs=out_specs, out_shape=out_shape,
        compiler_params=_params(("parallel",)),
        name="inproj_conv",
    )(h, h, norm_w, w, conv_w, conv_b)


def _memkv_kernel(mem_ref, nw_ref, w_ref, o_ref):
    xn = _rms(mem_ref[...], nw_ref[0]).astype(BF16)
    o_ref[0] = _dot(xn, w_ref[0].astype(BF16)).astype(BF16)


def _memkv(mem2d, mem_norm, w_kv):
    depth, d, n = w_kv.shape
    rows = mem2d.shape[0]
    return pl.pallas_call(
        _memkv_kernel,
        grid=(depth,),
        in_specs=[_resident((rows, d)),
                  pl.BlockSpec((1, 1, d), lambda i: (i, 0, 0)),
                  pl.BlockSpec((1, d, n), lambda i: (i, 0, 0))],
        out_specs=pl.BlockSpec((1, rows, n), lambda i: (i, 0, 0)),
        out_shape=jax.ShapeDtypeStruct((depth, rows, n), BF16),
        compiler_params=_params(("parallel",)),
        name="memkv",
    )(mem2d, mem_norm, w_kv)


def _gla_kernel(q_ref, k_ref, v_ref, g_ref, gl_ref, wg_ref, bg_ref, hn_ref, o_ref, st_ref):
    L = CHUNK

    @pl.when(pl.program_id(1) == 0)
    def _():
        st_ref[...] = jnp.zeros_like(st_ref)

    causal = _causal(L)
    tril = jnp.where(causal, 1.0, 0.0).astype(BF16)
    seqs = range(q_ref.shape[0])
    pre = [_dot(gl_ref[bi].astype(BF16), wg_ref[0]) + bg_ref[0] for bi in seqs]
    log2_a = [jnp.minimum(p, 0.0) * (LOG2E / GLA_GATE_TAU)
              - jnp.log2(1.0 + jnp.exp2(jnp.abs(p) * -LOG2E)) * (1.0 / GLA_GATE_TAU) for p in pre]
    b = [_cumsum_rows(tril, la) for la in log2_a]
    b_mid = [x[L // 2 - 1:L // 2, :] for x in b]
    b_last = [x[L - 1:L, :] for x in b]
    q = [q_ref[bi].astype(F32) * (GLA_HEAD_K ** -0.5) * jnp.exp2(b[bi] - b_mid[bi]) for bi in seqs]
    k = [k_ref[bi].astype(F32) * jnp.exp2(b_mid[bi] - b[bi]) for bi in seqs]
    q_mid = [x.astype(BF16) for x in q]
    k_mid = [x.astype(BF16) for x in k]
    q_dec = [(q[bi] * jnp.exp2(b_mid[bi])).astype(BF16) for bi in seqs]
    k_end = [(k[bi] * jnp.exp2(b_last[bi] - b_mid[bi])).astype(BF16) for bi in seqs]
    dec = [jnp.exp2(x) for x in b_last]
    for h in range(GLA_HEADS):
        ks = slice(h * GLA_HEAD_K, (h + 1) * GLA_HEAD_K)
        vs = slice(h * GLA_HEAD_V, (h + 1) * GLA_HEAD_V)
        for bi in seqs:
            s = jnp.where(causal, _dot_nt(q_mid[bi][:, ks], k_mid[bi][:, ks]), 0.0).astype(BF16)
            v = v_ref[bi, :, vs]
            st = st_ref[bi * GLA_HEADS + h]
            o = _dot(s, v) + _dot_nt(q_dec[bi][:, ks], st.astype(BF16))
            st_ref[bi * GLA_HEADS + h] = st * dec[bi][:, ks] + _dot_tn(v, k_end[bi][:, ks])
            g = g_ref[bi, :, vs].astype(F32)
            o_ref[bi, :, vs] = (_rms(o, hn_ref[0]) * _silu(g)).astype(BF16)


def _gla(proj, aux, wg, bg, hnorm, layer, batch, seq):
    t, n = proj.shape
    nb = 4 if batch % 4 == 0 else 1
    proj3 = proj.reshape(batch, seq, n)
    aux3 = aux.reshape(batch, seq, LANES)
    out = pl.pallas_call(
        _gla_kernel,
        grid=(batch // nb, seq // CHUNK),
        in_specs=[pl.BlockSpec((nb, CHUNK, GLA_DK), lambda b, c: (b, c, 0)),
                  pl.BlockSpec((nb, CHUNK, GLA_DK), lambda b, c: (b, c, 1)),
                  pl.BlockSpec((nb, CHUNK, GLA_DV), lambda b, c: (b, c, 1)),
                  pl.BlockSpec((nb, CHUNK, GLA_DV), lambda b, c: (b, c, 2)),
                  pl.BlockSpec((nb, CHUNK, LANES), lambda b, c: (b, c, 0)),
                  _layer(wg.shape, layer), _layer(bg.shape, layer), _layer(hnorm.shape, layer)],
        out_specs=pl.BlockSpec((nb, CHUNK, GLA_DV), lambda b, c: (b, c, 0)),
        out_shape=jax.ShapeDtypeStruct((batch, seq, GLA_DV), BF16),
        scratch_shapes=[pltpu.VMEM((nb * GLA_HEADS, GLA_HEAD_V, GLA_HEAD_K), F32)],
        compiler_params=_params(("parallel", "arbitrary")),
        name="gla",
    )(proj3, proj3, proj3, proj3, aux3, wg, bg, hnorm)
    return out.reshape(t, GLA_DV)


def _ssd_kernel(xs_ref, bm_ref, cm_ref, dt_ref, dtb_ref, alog_ref, dsk_ref, ex_ref, o_ref, st_ref, dtx_ref):
    L = CHUNK
    P = SSD_HEAD_DIM
    N = SSD_STATE

    @pl.when(pl.program_id(1) == 0)
    def _():
        st_ref[...] = jnp.zeros_like(st_ref)

    causal = _causal(L)
    tril = jnp.where(causal, 1.0, 0.0).astype(BF16)
    a = -jnp.exp(alog_ref[0])
    lane_lo = lax.broadcasted_iota(jnp.int32, (1, LANES), 1) < P
    seqs = range(xs_ref.shape[0])
    cs2, cs2_t, last2 = [], [], []
    for bi in seqs:
        dt = _softplus(dt_ref[bi] + dtb_ref[0])
        cs2.append(_cumsum_rows(tril, dt * a) * LOG2E)
        cs2_t.append(cs2[bi].T)
        last2.append(cs2[bi][L - 1:L, :])
        dt_hi = dt.astype(BF16)
        dt_lo = (dt - dt_hi.astype(F32)).astype(BF16)
        dtx_ref[bi] = _dot(jnp.concatenate([dt_hi, dt_lo], axis=1), ex_ref[...])

    for g in range(SSD_GROUPS):
        gs = slice(g * N, (g + 1) * N)
        cb = [jnp.where(causal, _dot_nt(cm_ref[bi, :, gs], bm_ref[bi, :, gs]), 0.0) for bi in seqs]
        for e2 in range(2):
            h0 = 4 * g + 2 * e2
            ps = slice((2 * g + e2) * LANES, (2 * g + e2 + 1) * LANES)
            ss = slice(e2 * LANES, (e2 + 1) * LANES)
            for bi in seqs:
                x_f = xs_ref[bi, :, ps].astype(F32)
                xd = x_f * dtx_ref[bi, :, ps]
                xd_b = xd.astype(BF16)
                st = st_ref[bi * SSD_GROUPS + g, :, ss]
                cols = [jnp.broadcast_to(cs2[bi][:, h:h + 1], (L, LANES)) for h in (h0, h0 + 1)]
                segs = [jnp.exp2(jnp.minimum(cols[e] - cs2_t[bi][h0 + e:h0 + e + 1, :], 0.0)) for e in range(2)]
                ys = [_dot((cb[bi] * seg).astype(BF16), xd_b) for seg in segs]
                col = jnp.where(lane_lo, cols[0], cols[1])
                last = jnp.where(lane_lo, last2[bi][:, h0:h0 + 1], last2[bi][:, h0 + 1:h0 + 2])
                y = (jnp.where(lane_lo, ys[0], ys[1])
                     + _dot(cm_ref[bi, :, gs], st.astype(BF16)) * jnp.exp2(col))
                o_ref[bi, :, ps] = (y + dsk_ref[0, :, ps] * x_f).astype(BF16)
                xw = (xd * jnp.exp2(last - col)).astype(BF16)
                st_ref[bi * SSD_GROUPS + g, :, ss] = st * jnp.exp2(last) + _dot_tn(bm_ref[bi, :, gs], xw)


def _ssd(proj, aux, dt_bias, a_log, d_skip, layer, batch, seq):
    t, n = proj.shape
    nb = 4 if batch % 4 == 0 else 1
    proj3 = proj.reshape(batch, seq, n)
    aux3 = aux.reshape(batch, seq, LANES)
    head_of = jnp.arange(SSD_D_INNER, dtype=jnp.int32) // SSD_HEAD_DIM
    expand = (jnp.arange(2 * LANES, dtype=jnp.int32)[:, None] % LANES == head_of[None, :]).astype(BF16)
    out = pl.pallas_call(
        _ssd_kernel,
        grid=(batch // nb, seq // CHUNK),
        in_specs=[pl.BlockSpec((nb, CHUNK, SSD_D_INNER), lambda b, c: (b, c, 1)),
                  pl.BlockSpec((nb, CHUNK, SSD_BC), lambda b, c: (b, c, 4)),
                  pl.BlockSpec((nb, CHUNK, SSD_BC), lambda b, c: (b, c, 5)),
                  pl.BlockSpec((nb, CHUNK, LANES), lambda b, c: (b, c, 0)),
                  _layer(dt_bias.shape, layer), _layer(a_log.shape, layer), _layer(d_skip.shape, layer),
                  _resident(expand.shape)],
        out_specs=pl.BlockSpec((nb, CHUNK, SSD_D_INNER), lambda b, c: (b, c, 0)),
        out_shape=jax.ShapeDtypeStruct((batch, seq, SSD_D_INNER), BF16),
        scratch_shapes=[pltpu.VMEM((nb * SSD_GROUPS, SSD_STATE, 4 * SSD_HEAD_DIM), F32),
                        pltpu.VMEM((nb, CHUNK, SSD_D_INNER), F32)],
        compiler_params=_params(("parallel", "arbitrary")),
        name="ssd",
    )(proj3, proj3, proj3, aux3, dt_bias, a_log, d_skip, expand)
    return out.reshape(t, SSD_D_INNER)


def _xattn_proj(xq_ref, kv_ref, wo_ref, d_mix):
    heads = range(XA_HEADS)
    cols = [slice(hd * XA_HEAD_DIM, (hd + 1) * XA_HEAD_DIM) for hd in heads]
    s = [_dot_nt(xq_ref[:, c], kv_ref[0, :, c]) * (XA_HEAD_DIM ** -0.5) for c in cols]
    p = [jnp.exp(x - jnp.max(x, axis=-1, keepdims=True)) for x in s]
    pv = [_dot(p[hd].astype(BF16), kv_ref[0, :, XA_WIDTH + hd * XA_HEAD_DIM:XA_WIDTH + (hd + 1) * XA_HEAD_DIM])
          / jnp.sum(p[hd], axis=-1, keepdims=True) for hd in heads]
    acc = None
    for hd in heads:
        term = _dot(pv[hd].astype(BF16), wo_ref[0, d_mix + hd * XA_HEAD_DIM:d_mix + (hd + 1) * XA_HEAD_DIM, :])
        acc = term if acc is None else acc + term
    return acc


def _xattn_out_kernel(mix_ref, xq_ref, kv_ref, wo_ref, h_ref, o_ref):
    d_mix = mix_ref.shape[1]
    xa = _xattn_proj(xq_ref, kv_ref, wo_ref, d_mix)
    o_ref[...] = h_ref[...] + xa + _dot(mix_ref[...], wo_ref[0, 0:d_mix, :])


def _xattn_out_gated_kernel(y_ref, z_ref, nw_ref, xq_ref, kv_ref, wo_ref, h_ref, o_ref, *, tk):
    d_mix = y_ref.shape[1]
    xa = _xattn_proj(xq_ref, kv_ref, wo_ref, d_mix)
    ssq = None
    acc = None
    for c in range(d_mix // tk):
        cs = slice(c * tk, (c + 1) * tk)
        t = y_ref[:, cs].astype(F32) * _silu(z_ref[:, cs].astype(F32))
        sq = jnp.sum(t * t, axis=-1, keepdims=True)
        term = _dot((t * nw_ref[0, :, cs]).astype(BF16), wo_ref[0, cs, :])
        ssq = sq if ssq is None else ssq + sq
        acc = term if acc is None else acc + term
    o_ref[...] = h_ref[...] + xa + acc * lax.rsqrt(ssq * (1.0 / d_mix) + EPS)


def _xattn_out(mix, proj, xq_block, memkv, layer, w_out, wlayer, h, seq, gate=None):
    t, d = h.shape
    tm = 1024 if gate is None and seq % 1024 == 0 else 512
    d_mix = mix.shape[1]
    mem_len = memkv.shape[1] // (t // seq)
    per_batch = seq // tm
    specs = [pl.BlockSpec((tm, XA_WIDTH), lambda i: (i, xq_block)),
             pl.BlockSpec((1, mem_len, 2 * XA_WIDTH), lambda i: (layer, i // per_batch, 0)),
             _layer(w_out.shape, wlayer),
             pl.BlockSpec((tm, d), lambda i: (i, 0))]
    mix_spec = pl.BlockSpec((tm, d_mix), lambda i: (i, 0))
    if gate is None:
        body, in_specs, args = _xattn_out_kernel, [mix_spec] + specs, (mix, proj, memkv, w_out, h)
    else:
        z_block, norm_w = gate
        body = functools.partial(_xattn_out_gated_kernel, tk=512)
        in_specs = [mix_spec, pl.BlockSpec((tm, d_mix), lambda i: (i, z_block)),
                    _layer(norm_w.shape, wlayer)] + specs
        args = (mix, proj, norm_w, proj, memkv, w_out, h)
    return pl.pallas_call(
        body,
        grid=(t // tm,),
        in_specs=in_specs,
        out_specs=pl.BlockSpec((tm, d), lambda i: (i, 0)),
        out_shape=jax.ShapeDtypeStruct((t, d), F32),
        compiler_params=_params(("parallel",)),
        name="xattn_out",
    )(*args)


def _ffn_kernel(h_ref, nw_ref, wi_ref, wo_ref, fnw_ref, o_ref, *, hidden, th, final):
    h = h_ref[...]
    xn = _rms(h, nw_ref[0]).astype(BF16)
    acc = h
    for c in range(hidden // th):
        gate = _dot(xn, wi_ref[0, :, c * th:(c + 1) * th])
        up = _dot(xn, wi_ref[0, :, hidden + c * th:hidden + (c + 1) * th])
        act = (_silu(gate) * up).astype(BF16)
        acc = acc + _dot(act, wo_ref[0, c * th:(c + 1) * th, :])
    if final:
        acc = _rms(acc, fnw_ref[...])
    o_ref[...] = acc


def _ffn(h, norm_w, w_in, w_out, layer, final_w, final, tm=512, th=256):
    t, d = h.shape
    hidden = w_out.shape[1]
    return pl.pallas_call(
        functools.partial(_ffn_kernel, hidden=hidden, th=th, final=final),
        grid=(t // tm,),
        in_specs=[pl.BlockSpec((tm, d), lambda i: (i, 0)),
                  _layer(norm_w.shape, layer), _layer(w_in.shape, layer), _layer(w_out.shape, layer),
                  _resident((1, d))],
        out_specs=pl.BlockSpec((tm, d), lambda i: (i, 0)),
        out_shape=jax.ShapeDtypeStruct((t, d), F32),
        compiler_params=_params(("parallel",)),
        name="ffn",
    )(h, norm_w, w_in, w_out, final_w)


def _rows(v, n=None):
    v = v[:, None, :]
    return v if n is None else jnp.pad(v, ((0, 0), (0, 0), (0, n - v.shape[2])))


def _reorder_in_proj(w, n_lead, n_small):
    small = jnp.pad(w[:, :, n_lead:n_lead + n_small], ((0, 0), (0, 0), (0, LANES - n_small)))
    return jnp.concatenate([w[:, :, :n_lead], w[:, :, n_lead + n_small:], small], axis=2).astype(BF16)


def kernel(x, mem, mix_norm, mem_norm, w_mem_kv, gla_w_in, gla_w_gate2, gla_b_gate, gla_head_norm, gla_w_out, ssd_w_in, ssd_conv_w, ssd_conv_b, ssd_dt_bias, ssd_a_log, ssd_d, ssd_norm, ssd_w_out, ffn_norm, ffn_w_in, ffn_w_out, final_norm):
    batch, seq, d = x.shape
    depth = mix_norm.shape[0]
    t = batch * seq
    h = x.reshape(t, d)

    gla_lead = 2 * GLA_DK + 2 * GLA_DV
    ssd_lead = SSD_D_INNER + SSD_CONV_DIM
    gla_w = _reorder_in_proj(gla_w_in, gla_lead, GLA_GATE_RANK)
    ssd_w = _reorder_in_proj(ssd_w_in, ssd_lead, SSD_HEADS)
    gla_wg = jnp.pad(gla_w_gate2, ((0, 0), (0, LANES - GLA_GATE_RANK), (0, 0))).astype(BF16)
    gla_wo = gla_w_out.astype(BF16)
    ssd_wo = ssd_w_out.astype(BF16)
    ffn_wi = ffn_w_in.astype(BF16)
    ffn_wo = ffn_w_out.astype(BF16)
    mix_nw, ffn_nw = _rows(mix_norm), _rows(ffn_norm)
    gla_bg, gla_hn = _rows(gla_b_gate), _rows(gla_head_norm)
    ssd_cb, ssd_nw = _rows(ssd_conv_b), _rows(ssd_norm)
    ssd_dtb, ssd_alog = _rows(ssd_dt_bias, LANES), _rows(ssd_a_log, LANES)
    ssd_dsk = _rows(jnp.repeat(ssd_d, SSD_HEAD_DIM, axis=1))
    final_w = final_norm.reshape(1, d)

    memkv = _memkv(mem.reshape(-1, d), _rows(mem_norm), w_mem_kv)

    for i in range(depth):
        j = i // 2
        if i % 2 == 0:
            proj, aux = _inproj(h, mix_nw, gla_w, i, j, gla_lead + XA_WIDTH)
            mix = _gla(proj, aux, gla_wg, gla_bg, gla_hn, j, batch, seq)
            h = _xattn_out(mix, proj, gla_lead // XA_WIDTH, memkv, i, gla_wo, j, h, seq)
        else:
            proj, aux = _inproj(h, mix_nw, ssd_w, i, j, ssd_lead + XA_WIDTH,
                                conv=(ssd_conv_w, ssd_cb, SSD_D_INNER), seq=seq)
            y = _ssd(proj, aux, ssd_dtb, ssd_alog, ssd_dsk, j, batch, seq)
            h = _xattn_out(y, proj, ssd_lead // XA_WIDTH, memkv, i, ssd_wo, j, h, seq,
                           gate=(0, ssd_nw))
        h = _ffn(h, ffn_nw, ffn_wi, ffn_wo, i, final_w, final=(i == depth - 1))
    return h.reshape(batch, seq, d)
```

```python
import functools

import jax
import jax.numpy as jnp
from jax import lax
from jax.experimental import pallas as pl
from jax.experimental.pallas import tpu as pltpu

F32 = jnp.float32
BF16 = jnp.bfloat16
EPS = 1e-6
LOG2E = 1.4426950408889634

LANES = 128
BF16_ROWS = 16
XA_HEADS = 4
XA_HEAD_DIM = 256
XA_WIDTH = XA_HEADS * XA_HEAD_DIM
GLA_HEADS = 4
GLA_HEAD_K = 128
GLA_HEAD_V = 256
GLA_DK = GLA_HEADS * GLA_HEAD_K
GLA_DV = GLA_HEADS * GLA_HEAD_V
GLA_GATE_RANK = 16
GLA_GATE_TAU = 16.0
SSD_D_INNER = 2048
SSD_HEAD_DIM = 64
SSD_HEADS = 32
SSD_GROUPS = 8
SSD_STATE = 128
SSD_CONV_K = 4
SSD_BC = SSD_GROUPS * SSD_STATE
SSD_CONV_DIM = SSD_D_INNER + 2 * SSD_BC
CHUNK = 128
VMEM_LIMIT = 56 * 1024 * 1024

NT = (((1,), (1,)), ((), ()))
TN = (((0,), (0,)), ((), ()))


def _dot(a, b):
    return jnp.dot(a, b, preferred_element_type=F32)


def _dot_nt(a, b):
    return lax.dot_general(a, b, NT, preferred_element_type=F32)


def _dot_tn(a, b):
    return lax.dot_general(a, b, TN, preferred_element_type=F32)


def _rms(x, w):
    return x * lax.rsqrt(jnp.mean(x * x, axis=-1, keepdims=True) + EPS) * w


def _silu(x):
    hx = 0.5 * x
    return hx + hx * jnp.tanh(hx)


def _softplus(x):
    return jnp.maximum(x, 0.0) + jnp.log(1.0 + jnp.exp(-jnp.abs(x)))


def _causal(n):
    row = lax.broadcasted_iota(jnp.int32, (n, n), 0)
    col = lax.broadcasted_iota(jnp.int32, (n, n), 1)
    return row >= col


def _cumsum_rows(tril, x):
    x1 = x.astype(BF16)
    r1 = x - x1.astype(F32)
    x2 = r1.astype(BF16)
    x3 = (r1 - x2.astype(F32)).astype(BF16)
    return _dot(tril, x1) + _dot(tril, x2) + _dot(tril, x3)


def _params(sem):
    return pltpu.CompilerParams(dimension_semantics=sem, vmem_limit_bytes=VMEM_LIMIT)


def _resident(shape):
    nd = len(shape)
    return pl.BlockSpec(shape, lambda *_: (0,) * nd, pipeline_mode=pl.Buffered(1))


def _layer(shape, layer):
    nd = len(shape)
    return pl.BlockSpec((1,) + tuple(shape[1:]), lambda *_: (layer,) + (0,) * (nd - 1),
                        pipeline_mode=pl.Buffered(1))


def _inproj_kernel(x_ref, nw_ref, w_ref, main_ref, aux_ref, *, n_main, tn):
    xn = _rms(x_ref[...], nw_ref[0]).astype(BF16)
    for c in range(n_main // tn):
        cs = slice(c * tn, (c + 1) * tn)
        main_ref[:, cs] = _dot(xn, w_ref[0, :, cs]).astype(BF16)
    aux_ref[...] = _dot(xn, w_ref[0, :, n_main:])


def _inproj_conv_kernel(x_ref, xp_ref, nw_ref, w_ref, cw_ref, cb_ref, main_ref, aux_ref, *,
                        n_main, conv_lo, conv_hi, tn, blocks_per_seq):
    tm = x_ref.shape[0]
    hist = xp_ref.shape[0]
    xn = _rms(x_ref[...], nw_ref[0]).astype(BF16)
    keep = (pl.program_id(0) % blocks_per_seq != 0).astype(F32)
    xp = (_rms(xp_ref[...], nw_ref[0]) * keep).astype(BF16)
    lhs = jnp.concatenate([xp, xn], axis=0)
    sub = lax.broadcasted_iota(jnp.int32, (1, 8, tn), 1)
    is_conv = lambda c: conv_lo <= c * tn < conv_hi
    conv_chunks = [c for c in range(n_main // tn) if is_conv(c)]
    plain_chunks = [c for c in range(n_main // tn) if not is_conv(c)]
    order = []
    for k in range(max(len(conv_chunks), len(plain_chunks))):
        order += conv_chunks[k:k + 1] + plain_chunks[k:k + 1]
    for c in order:
        cs = slice(c * tn, (c + 1) * tn)
        w = w_ref[0, :, cs]
        if is_conv(c):
            cc = slice(c * tn - conv_lo, (c + 1) * tn - conv_lo)
            r = _dot(lhs, w).reshape((hist + tm) // 8, 8, tn)
            hw = 0.5 * cw_ref[0, :, cc]
            acc = 0.5 * cb_ref[0, :, cc] + r[hist // 8:] * hw[SSD_CONV_K - 1:SSD_CONV_K, :]
            for s in range(1, SSD_CONV_K):
                rr = pltpu.roll(r, s, 1)
                shifted = jnp.where(sub < s, rr[hist // 8 - 1:-1], rr[hist // 8:])
                acc = acc + shifted * hw[SSD_CONV_K - 1 - s:SSD_CONV_K - s, :]
            act = acc + acc * jnp.tanh(acc)
            main_ref[:, cs] = act.reshape(tm, tn).astype(BF16)
        else:
            main_ref[:, cs] = _dot(xn, w).astype(BF16)
    aux_ref[...] = _dot(xn, w_ref[0, :, n_main:])


def _inproj(h, norm_w, w, layer, wlayer, n_main, conv=None, seq=None, tm=512, tn=512):
    t, d = h.shape
    n = w.shape[2]
    out_specs = [pl.BlockSpec((tm, n_main), lambda i: (i, 0)),
                 pl.BlockSpec((tm, n - n_main), lambda i: (i, 0))]
    out_shape = [jax.ShapeDtypeStruct((t, n_main), BF16),
                 jax.ShapeDtypeStruct((t, n - n_main), F32)]
    x_spec = pl.BlockSpec((tm, d), lambda i: (i, 0))
    if conv is None:
        return pl.pallas_call(
            functools.partial(_inproj_kernel, n_main=n_main, tn=tn),
            grid=(t // tm,),
            in_specs=[x_spec, _layer(norm_w.shape, layer), _layer(w.shape, wlayer)],
            out_specs=out_specs, out_shape=out_shape,
            compiler_params=_params(("parallel",)),
            name="inproj",
        )(h, norm_w, w)
    conv_w, conv_b, conv_lo = conv
    per = tm // BF16_ROWS
    prev_spec = pl.BlockSpec((BF16_ROWS, d), lambda i: (jnp.maximum(i * per - 1, 0), 0))
    return pl.pallas_call(
        functools.partial(_inproj_conv_kernel, n_main=n_main, conv_lo=conv_lo,
                          conv_hi=conv_lo + conv_w.shape[2], tn=tn, blocks_per_seq=seq // tm),
        grid=(t // tm,),
        in_specs=[x_spec, prev_spec, _layer(norm_w.shape, layer), _layer(w.shape, wlayer),
                  _layer(conv_w.shape, wlayer), _layer(conv_b.shape, wlayer)],
        out_specs=out_specs, out_shape=out_shape,
        compiler_params=_params(("parallel",)),
        name="inproj_conv",
    )(h, h, norm_w, w, conv_w, conv_b)


def _memkv_kernel(mem_ref, nw_ref, w_ref, o_ref):
    xn = _rms(mem_ref[...], nw_ref[0]).astype(BF16)
    o_ref[0] = _dot(xn, w_ref[0].astype(BF16)).astype(BF16)


def _memkv(mem2d, mem_norm, w_kv):
    depth, d, n = w_kv.shape
    rows = mem2d.shape[0]
    return pl.pallas_call(
        _memkv_kernel,
        grid=(depth,),
        in_specs=[_resident((rows, d)),
                  pl.BlockSpec((1, 1, d), lambda i: (i, 0, 0)),
                  pl.BlockSpec((1, d, n), lambda i: (i, 0, 0))],
        out_specs=pl.BlockSpec((1, rows, n), lambda i: (i, 0, 0)),
        out_shape=jax.ShapeDtypeStruct((depth, rows, n), BF16),
        compiler_params=_params(("parallel",)),
        name="memkv",
    )(mem2d, mem_norm, w_kv)


def _gla_kernel(q_ref, k_ref, v_ref, g_ref, gl_ref, wg_ref, bg_ref, hn_ref, o_ref, st_ref):
    L = CHUNK

    @pl.when(pl.program_id(1) == 0)
    def _():
        st_ref[...] = jnp.zeros_like(st_ref)

    causal = _causal(L)
    tril = jnp.where(causal, 1.0, 0.0).astype(BF16)
    seqs = range(q_ref.shape[0])
    pre = [_dot(gl_ref[bi].astype(BF16), wg_ref[0]) + bg_ref[0] for bi in seqs]
    log2_a = [jnp.minimum(p, 0.0) * (LOG2E / GLA_GATE_TAU)
              - jnp.log2(1.0 + jnp.exp2(jnp.abs(p) * -LOG2E)) * (1.0 / GLA_GATE_TAU) for p in pre]
    b = [_cumsum_rows(tril, la) for la in log2_a]
    b_mid = [x[L // 2 - 1:L // 2, :] for x in b]
    b_last = [x[L - 1:L, :] for x in b]
    q = [q_ref[bi].astype(F32) * (GLA_HEAD_K ** -0.5) * jnp.exp2(b[bi] - b_mid[bi]) for bi in seqs]
    k = [k_ref[bi].astype(F32) * jnp.exp2(b_mid[bi] - b[bi]) for bi in seqs]
    q_mid = [x.astype(BF16) for x in q]
    k_mid = [x.astype(BF16) for x in k]
    q_dec = [(q[bi] * jnp.exp2(b_mid[bi])).astype(BF16) for bi in seqs]
    k_end = [(k[bi] * jnp.exp2(b_last[bi] - b_mid[bi])).astype(BF16) for bi in seqs]
    dec = [jnp.exp2(x) for x in b_last]
    for h in range(GLA_HEADS):
        ks = slice(h * GLA_HEAD_K, (h + 1) * GLA_HEAD_K)
        vs = slice(h * GLA_HEAD_V, (h + 1) * GLA_HEAD_V)
        s = [jnp.where(causal, _dot_nt(q_mid[bi][:, ks], k_mid[bi][:, ks]), 0.0).astype(BF16) for bi in seqs]
        st = [st_ref[bi * GLA_HEADS + h] for bi in seqs]
        o = [_dot(s[bi], v_ref[bi, :, vs]) + _dot_nt(q_dec[bi][:, ks], st[bi].astype(BF16)) for bi in seqs]
        for bi in seqs:
            st_ref[bi * GLA_HEADS + h] = st[bi] * dec[bi][:, ks] + _dot_tn(v_ref[bi, :, vs], k_end[bi][:, ks])
        for bi in seqs:
            g = g_ref[bi, :, vs].astype(F32)
            o_ref[bi, :, vs] = (_rms(o[bi], hn_ref[0]) * _silu(g)).astype(BF16)


def _gla(proj, aux, wg, bg, hnorm, layer, batch, seq):
    t, n = proj.shape
    nb = 4 if batch % 4 == 0 else 1
    proj3 = proj.reshape(batch, seq, n)
    aux3 = aux.reshape(batch, seq, LANES)
    out = pl.pallas_call(
        _gla_kernel,
        grid=(batch // nb, seq // CHUNK),
        in_specs=[pl.BlockSpec((nb, CHUNK, GLA_DK), lambda b, c: (b, c, 0)),
                  pl.BlockSpec((nb, CHUNK, GLA_DK), lambda b, c: (b, c, 1)),
                  pl.BlockSpec((nb, CHUNK, GLA_DV), lambda b, c: (b, c, 1)),
                  pl.BlockSpec((nb, CHUNK, GLA_DV), lambda b, c: (b, c, 2)),
                  pl.BlockSpec((nb, CHUNK, LANES), lambda b, c: (b, c, 0)),
                  _layer(wg.shape, layer), _layer(bg.shape, layer), _layer(hnorm.shape, layer)],
        out_specs=pl.BlockSpec((nb, CHUNK, GLA_DV), lambda b, c: (b, c, 0)),
        out_shape=jax.ShapeDtypeStruct((batch, seq, GLA_DV), BF16),
        scratch_shapes=[pltpu.VMEM((nb * GLA_HEADS, GLA_HEAD_V, GLA_HEAD_K), F32)],
        compiler_params=_params(("parallel", "arbitrary")),
        name="gla",
    )(proj3, proj3, proj3, proj3, aux3, wg, bg, hnorm)
    return out.reshape(t, GLA_DV)


def _ssd_kernel(xs_ref, bm_ref, cm_ref, dt_ref, dtb_ref, alog_ref, dsk_ref, ex_ref, o_ref, st_ref, dtx_ref):
    L = CHUNK
    P = SSD_HEAD_DIM
    N = SSD_STATE

    @pl.when(pl.program_id(1) == 0)
    def _():
        st_ref[...] = jnp.zeros_like(st_ref)

    causal = _causal(L)
    tril = jnp.where(causal, 1.0, 0.0).astype(BF16)
    a = -jnp.exp(alog_ref[0])
    lane_lo = lax.broadcasted_iota(jnp.int32, (1, LANES), 1) < P
    seqs = range(xs_ref.shape[0])
    cs2, cs2_t, last2 = [], [], []
    for bi in seqs:
        dt = _softplus(dt_ref[bi] + dtb_ref[0])
        cs2.append(_cumsum_rows(tril, dt * a) * LOG2E)
        cs2_t.append(cs2[bi].T)
        last2.append(cs2[bi][L - 1:L, :])
        dt_hi = dt.astype(BF16)
        dt_lo = (dt - dt_hi.astype(F32)).astype(BF16)
        dtx_ref[bi] = _dot(jnp.concatenate([dt_hi, dt_lo], axis=1), ex_ref[...])

    for g in range(SSD_GROUPS):
        gs = slice(g * N, (g + 1) * N)
        cb = [jnp.where(causal, _dot_nt(cm_ref[bi, :, gs], bm_ref[bi, :, gs]), 0.0) for bi in seqs]
        for e2 in range(2):
            h0 = 4 * g + 2 * e2
            ps = slice((2 * g + e2) * LANES, (2 * g + e2 + 1) * LANES)
            ss = slice(e2 * LANES, (e2 + 1) * LANES)
            for bi in seqs:
                x_f = xs_ref[bi, :, ps].astype(F32)
                xd = x_f * dtx_ref[bi, :, ps]
                xd_b = xd.astype(BF16)
                st = st_ref[bi * SSD_GROUPS + g, :, ss]
                cols = [jnp.broadcast_to(cs2[bi][:, h:h + 1], (L, LANES)) for h in (h0, h0 + 1)]
                segs = [jnp.exp2(jnp.minimum(cols[e] - cs2_t[bi][h0 + e:h0 + e + 1, :], 0.0)) for e in range(2)]
                ys = [_dot((cb[bi] * seg).astype(BF16), xd_b) for seg in segs]
                col = jnp.where(lane_lo, cols[0], cols[1])
                last = jnp.where(lane_lo, last2[bi][:, h0:h0 + 1], last2[bi][:, h0 + 1:h0 + 2])
                y = (jnp.where(lane_lo, ys[0], ys[1])
                     + _dot(cm_ref[bi, :, gs], st.astype(BF16)) * jnp.exp2(col))
                o_ref[bi, :, ps] = (y + dsk_ref[0, :, ps] * x_f).astype(BF16)
                xw = (xd * jnp.exp2(last - col)).astype(BF16)
                st_ref[bi * SSD_GROUPS + g, :, ss] = st * jnp.exp2(last) + _dot_tn(bm_ref[bi, :, gs], xw)


def _ssd(proj, aux, dt_bias, a_log, d_skip, layer, batch, seq):
    t, n = proj.shape
    nb = 4 if batch % 4 == 0 else 1
    proj3 = proj.reshape(batch, seq, n)
    aux3 = aux.reshape(batch, seq, LANES)
    head_of = jnp.arange(SSD_D_INNER, dtype=jnp.int32) // SSD_HEAD_DIM
    expand = (jnp.arange(2 * LANES, dtype=jnp.int32)[:, None] % LANES == head_of[None, :]).astype(BF16)
    out = pl.pallas_call(
        _ssd_kernel,
        grid=(batch // nb, seq // CHUNK),
        in_specs=[pl.BlockSpec((nb, CHUNK, SSD_D_INNER), lambda b, c: (b, c, 1)),
                  pl.BlockSpec((nb, CHUNK, SSD_BC), lambda b, c: (b, c, 4)),
                  pl.BlockSpec((nb, CHUNK, SSD_BC), lambda b, c: (b, c, 5)),
                  pl.BlockSpec((nb, CHUNK, LANES), lambda b, c: (b, c, 0)),
                  _layer(dt_bias.shape, layer), _layer(a_log.shape, layer), _layer(d_skip.shape, layer),
                  _resident(expand.shape)],
        out_specs=pl.BlockSpec((nb, CHUNK, SSD_D_INNER), lambda b, c: (b, c, 0)),
        out_shape=jax.ShapeDtypeStruct((batch, seq, SSD_D_INNER), BF16),
        scratch_shapes=[pltpu.VMEM((nb * SSD_GROUPS, SSD_STATE, 4 * SSD_HEAD_DIM), F32),
                        pltpu.VMEM((nb, CHUNK, SSD_D_INNER), F32)],
        compiler_params=_params(("parallel", "arbitrary")),
        name="ssd",
    )(proj3, proj3, proj3, aux3, dt_bias, a_log, d_skip, expand)
    return out.reshape(t, SSD_D_INNER)


def _xattn_proj(xq_ref, kv_ref, wo_ref, d_mix):
    heads = range(XA_HEADS)
    cols = [slice(hd * XA_HEAD_DIM, (hd + 1) * XA_HEAD_DIM) for hd in heads]
    s = [_dot_nt(xq_ref[:, c], kv_ref[0, :, c]) * (XA_HEAD_DIM ** -0.5) for c in cols]
    p = [jnp.exp(x - jnp.max(x, axis=-1, keepdims=True)) for x in s]
    pv = [_dot(p[hd].astype(BF16), kv_ref[0, :, XA_WIDTH + hd * XA_HEAD_DIM:XA_WIDTH + (hd + 1) * XA_HEAD_DIM])
          / jnp.sum(p[hd], axis=-1, keepdims=True) for hd in heads]
    acc = None
    for hd in heads:
        w = wo_ref[0, d_mix + hd * XA_HEAD_DIM:d_mix + (hd + 1) * XA_HEAD_DIM, :].astype(BF16)
        term = _dot(pv[hd].astype(BF16), w)
        acc = term if acc is None else acc + term
    return acc


def _xattn_out_kernel(mix_ref, xq_ref, kv_ref, wo_ref, h_ref, o_ref):
    d_mix = mix_ref.shape[1]
    xa = _xattn_proj(xq_ref, kv_ref, wo_ref, d_mix)
    o_ref[...] = h_ref[...] + xa + _dot(mix_ref[...], wo_ref[0, 0:d_mix, :].astype(BF16))


def _xattn_out_gated_kernel(y_ref, z_ref, nw_ref, xq_ref, kv_ref, wo_ref, h_ref, o_ref, *, tk):
    d_mix = y_ref.shape[1]
    xa = _xattn_proj(xq_ref, kv_ref, wo_ref, d_mix)
    ssq = None
    acc = None
    for c in range(d_mix // tk):
        cs = slice(c * tk, (c + 1) * tk)
        t = y_ref[:, cs].astype(F32) * _silu(z_ref[:, cs].astype(F32))
        sq = jnp.sum(t * t, axis=-1, keepdims=True)
        term = _dot((t * nw_ref[0, :, cs]).astype(BF16), wo_ref[0, cs, :].astype(BF16))
        ssq = sq if ssq is None else ssq + sq
        acc = term if acc is None else acc + term
    o_ref[...] = h_ref[...] + xa + acc * lax.rsqrt(ssq * (1.0 / d_mix) + EPS)


def _xattn_out(mix, proj, xq_block, memkv, layer, w_out, wlayer, h, seq, gate=None):
    t, d = h.shape
    tm = 1024 if gate is None and seq % 1024 == 0 else 512
    d_mix = mix.shape[1]
    mem_len = memkv.shape[1] // (t // seq)
    per_batch = seq // tm
    specs = [pl.BlockSpec((tm, XA_WIDTH), lambda i: (i, xq_block)),
             pl.BlockSpec((1, mem_len, 2 * XA_WIDTH), lambda i: (layer, i // per_batch, 0)),
             _layer(w_out.shape, wlayer),
             pl.BlockSpec((tm, d), lambda i: (i, 0))]
    mix_spec = pl.BlockSpec((tm, d_mix), lambda i: (i, 0))
    if gate is None:
        body, in_specs, args = _xattn_out_kernel, [mix_spec] + specs, (mix, proj, memkv, w_out, h)
    else:
        z_block, norm_w = gate
        body = functools.partial(_xattn_out_gated_kernel, tk=512)
        in_specs = [mix_spec, pl.BlockSpec((tm, d_mix), lambda i: (i, z_block)),
                    _layer(norm_w.shape, wlayer)] + specs
        args = (mix, proj, norm_w, proj, memkv, w_out, h)
    return pl.pallas_call(
        body,
        grid=(t // tm,),
        in_specs=in_specs,
        out_specs=pl.BlockSpec((tm, d), lambda i: (i, 0)),
        out_shape=jax.ShapeDtypeStruct((t, d), F32),
        compiler_params=_params(("parallel",)),
        name="xattn_out",
    )(*args)


def _ffn_kernel(h_ref, nw_ref, wi_ref, wo_ref, fnw_ref, o_ref, *, hidden, th, final):
    h = h_ref[...]
    xn = _rms(h, nw_ref[0]).astype(BF16)
    acc = h
    for c in range(hidden // th):
        gate = _dot(xn, wi_ref[0, :, c * th:(c + 1) * th].astype(BF16))
        up = _dot(xn, wi_ref[0, :, hidden + c * th:hidden + (c + 1) * th].astype(BF16))
        act = (_silu(gate) * up).astype(BF16)
        acc = acc + _dot(act, wo_ref[0, c * th:(c + 1) * th, :].astype(BF16))
    if final:
        acc = _rms(acc, fnw_ref[...])
    o_ref[...] = acc


def _ffn(h, norm_w, w_in, w_out, layer, final_w, final, tm=512, th=256):
    t, d = h.shape
    hidden = w_out.shape[1]
    return pl.pallas_call(
        functools.partial(_ffn_kernel, hidden=hidden, th=th, final=final),
        grid=(t // tm,),
        in_specs=[pl.BlockSpec((tm, d), lambda i: (i, 0)),
                  _layer(norm_w.shape, layer), _layer(w_in.shape, layer), _layer(w_out.shape, layer),
                  _resident((1, d))],
        out_specs=pl.BlockSpec((tm, d), lambda i: (i, 0)),
        out_shape=jax.ShapeDtypeStruct((t, d), F32),
        compiler_params=_params(("parallel",)),
        name="ffn",
    )(h, norm_w, w_in, w_out, final_w)


def _rows(v, n=None):
    v = v[:, None, :]
    return v if n is None else jnp.pad(v, ((0, 0), (0, 0), (0, n - v.shape[2])))


def _reorder_in_proj(w, n_lead, n_small):
    small = jnp.pad(w[:, :, n_lead:n_lead + n_small], ((0, 0), (0, 0), (0, LANES - n_small)))
    return jnp.concatenate([w[:, :, :n_lead], w[:, :, n_lead + n_small:], small], axis=2).astype(BF16)


def kernel(x, mem, mix_norm, mem_norm, w_mem_kv, gla_w_in, gla_w_gate2, gla_b_gate, gla_head_norm, gla_w_out, ssd_w_in, ssd_conv_w, ssd_conv_b, ssd_dt_bias, ssd_a_log, ssd_d, ssd_norm, ssd_w_out, ffn_norm, ffn_w_in, ffn_w_out, final_norm):
    batch, seq, d = x.shape
    depth = mix_norm.shape[0]
    t = batch * seq
    h = x.reshape(t, d)

    gla_lead = 2 * GLA_DK + 2 * GLA_DV
    ssd_lead = SSD_D_INNER + SSD_CONV_DIM
    gla_w = _reorder_in_proj(gla_w_in, gla_lead, GLA_GATE_RANK)
    ssd_w = _reorder_in_proj(ssd_w_in, ssd_lead, SSD_HEADS)
    gla_wg = jnp.pad(gla_w_gate2, ((0, 0), (0, LANES - GLA_GATE_RANK), (0, 0))).astype(BF16)
    mix_nw, ffn_nw = _rows(mix_norm), _rows(ffn_norm)
    gla_bg, gla_hn = _rows(gla_b_gate), _rows(gla_head_norm)
    ssd_cb, ssd_nw = _rows(ssd_conv_b), _rows(ssd_norm)
    ssd_dtb, ssd_alog = _rows(ssd_dt_bias, LANES), _rows(ssd_a_log, LANES)
    ssd_dsk = _rows(jnp.repeat(ssd_d, SSD_HEAD_DIM, axis=1))
    final_w = final_norm.reshape(1, d)

    memkv = _memkv(mem.reshape(-1, d), _rows(mem_norm), w_mem_kv)

    for i in range(depth):
        j = i // 2
        if i % 2 == 0:
            proj, aux = _inproj(h, mix_nw, gla_w, i, j, gla_lead + XA_WIDTH)
            mix = _gla(proj, aux, gla_wg, gla_bg, gla_hn, j, batch, seq)
            h = _xattn_out(mix, proj, gla_lead // XA_WIDTH, memkv, i, gla_w_out, j, h, seq)
        else:
            proj, aux = _inproj(h, mix_nw, ssd_w, i, j, ssd_lead + XA_WIDTH,
                                conv=(ssd_conv_w, ssd_cb, SSD_D_INNER), seq=seq)
            y = _ssd(proj, aux, ssd_dtb, ssd_alog, ssd_dsk, j, batch, seq)
            h = _xattn_out(y, proj, ssd_lead // XA_WIDTH, memkv, i, ssd_w_out, j, h, seq,
                           gate=(0, ssd_nw))
        h = _ffn(h, ffn_nw, ffn_w_in, ffn_w_out, i, final_w, final=(i == depth - 1))
    return h.reshape(batch, seq, d)
```

```python
import functools

import jax
import jax.numpy as jnp
from jax import lax
from jax.experimental import pallas as pl
from jax.experimental.pallas import tpu as pltpu

F32 = jnp.float32
BF16 = jnp.bfloat16
EPS = 1e-6
LOG2E = 1.4426950408889634

LANES = 128
BF16_ROWS = 16
XA_HEADS = 4
XA_HEAD_DIM = 256
XA_WIDTH = XA_HEADS * XA_HEAD_DIM
GLA_HEADS = 4
GLA_HEAD_K = 128
GLA_HEAD_V = 256
GLA_DK = GLA_HEADS * GLA_HEAD_K
GLA_DV = GLA_HEADS * GLA_HEAD_V
GLA_GATE_RANK = 16
GLA_GATE_TAU = 16.0
SSD_D_INNER = 2048
SSD_HEAD_DIM = 64
SSD_HEADS = 32
SSD_GROUPS = 8
SSD_STATE = 128
SSD_CONV_K = 4
SSD_BC = SSD_GROUPS * SSD_STATE
SSD_CONV_DIM = SSD_D_INNER + 2 * SSD_BC
CHUNK = 128
VMEM_LIMIT = 56 * 1024 * 1024

NT = (((1,), (1,)), ((), ()))
TN = (((0,), (0,)), ((), ()))


def _dot(a, b):
    return jnp.dot(a, b, preferred_element_type=F32)


def _dot_nt(a, b):
    return lax.dot_general(a, b, NT, preferred_element_type=F32)


def _dot_tn(a, b):
    return lax.dot_general(a, b, TN, preferred_element_type=F32)


def _rms(x, w):
    return x * lax.rsqrt(jnp.mean(x * x, axis=-1, keepdims=True) + EPS) * w


def _silu(x):
    hx = 0.5 * x
    return hx + hx * jnp.tanh(hx)


def _softplus(x):
    return jnp.maximum(x, 0.0) + jnp.log(1.0 + jnp.exp(-jnp.abs(x)))


def _causal(n):
    row = lax.broadcasted_iota(jnp.int32, (n, n), 0)
    col = lax.broadcasted_iota(jnp.int32, (n, n), 1)
    return row >= col


def _cumsum_rows(tril, x):
    x1 = x.astype(BF16)
    r1 = x - x1.astype(F32)
    x2 = r1.astype(BF16)
    x3 = (r1 - x2.astype(F32)).astype(BF16)
    return _dot(tril, x1) + _dot(tril, x2) + _dot(tril, x3)


def _params(sem):
    return pltpu.CompilerParams(dimension_semantics=sem, vmem_limit_bytes=VMEM_LIMIT)


def _resident(shape):
    nd = len(shape)
    return pl.BlockSpec(shape, lambda *_: (0,) * nd, pipeline_mode=pl.Buffered(1))


def _layer(shape, layer):
    nd = len(shape)
    return pl.BlockSpec((1,) + tuple(shape[1:]), lambda *_: (layer,) + (0,) * (nd - 1),
                        pipeline_mode=pl.Buffered(1))


def _inproj_kernel(x_ref, nw_ref, w_ref, main_ref, aux_ref, *, n_main, tn):
    xn = _rms(x_ref[...], nw_ref[0]).astype(BF16)
    for c in range(n_main // tn):
        cs = slice(c * tn, (c + 1) * tn)
        main_ref[:, cs] = _dot(xn, w_ref[0, :, cs]).astype(BF16)
    aux_ref[...] = _dot(xn, w_ref[0, :, n_main:])


def _inproj_conv_kernel(x_ref, xp_ref, nw_ref, w_ref, cw_ref, cb_ref, main_ref, aux_ref, *,
                        n_main, conv_lo, conv_hi, tn, blocks_per_seq):
    tm = x_ref.shape[0]
    hist = xp_ref.shape[0]
    xn = _rms(x_ref[...], nw_ref[0]).astype(BF16)
    keep = (pl.program_id(0) % blocks_per_seq != 0).astype(F32)
    xp = (_rms(xp_ref[...], nw_ref[0]) * keep).astype(BF16)
    lhs = jnp.concatenate([xp, xn], axis=0)
    sub = lax.broadcasted_iota(jnp.int32, (1, 8, tn), 1)
    is_conv = lambda c: conv_lo <= c * tn < conv_hi
    conv_chunks = [c for c in range(n_main // tn) if is_conv(c)]
    plain_chunks = [c for c in range(n_main // tn) if not is_conv(c)]
    order = []
    for k in range(max(len(conv_chunks), len(plain_chunks))):
        order += conv_chunks[k:k + 1] + plain_chunks[k:k + 1]
    for c in order:
        cs = slice(c * tn, (c + 1) * tn)
        w = w_ref[0, :, cs]
        if is_conv(c):
            cc = slice(c * tn - conv_lo, (c + 1) * tn - conv_lo)
            r = _dot(lhs, w).reshape((hist + tm) // 8, 8, tn)
            hw = 0.5 * cw_ref[0, :, cc]
            acc = 0.5 * cb_ref[0, :, cc] + r[hist // 8:] * hw[SSD_CONV_K - 1:SSD_CONV_K, :]
            for s in range(1, SSD_CONV_K):
                rr = pltpu.roll(r, s, 1)
                shifted = jnp.where(sub < s, rr[hist // 8 - 1:-1], rr[hist // 8:])
                acc = acc + shifted * hw[SSD_CONV_K - 1 - s:SSD_CONV_K - s, :]
            act = acc + acc * jnp.tanh(acc)
            main_ref[:, cs] = act.reshape(tm, tn).astype(BF16)
        else:
            main_ref[:, cs] = _dot(xn, w).astype(BF16)
    aux_ref[...] = _dot(xn, w_ref[0, :, n_main:])


def _inproj(h, norm_w, w, layer, wlayer, n_main, conv=None, seq=None, tm=512, tn=512):
    t, d = h.shape
    n = w.shape[2]
    out_specs = [pl.BlockSpec((tm, n_main), lambda i: (i, 0)),
                 pl.BlockSpec((tm, n - n_main), lambda i: (i, 0))]
    out_shape = [jax.ShapeDtypeStruct((t, n_main), BF16),
                 jax.ShapeDtypeStruct((t, n - n_main), F32)]
    x_spec = pl.BlockSpec((tm, d), lambda i: (i, 0))
    if conv is None:
        return pl.pallas_call(
            functools.partial(_inproj_kernel, n_main=n_main, tn=tn),
            grid=(t // tm,),
            in_specs=[x_spec, _layer(norm_w.shape, layer), _layer(w.shape, wlayer)],
            out_specs=out_specs, out_shape=out_shape,
            compiler_params=_params(("parallel",)),
            name="inproj",
        )(h, norm_w, w)
    conv_w, conv_b, conv_lo = conv
    per = tm // BF16_ROWS
    prev_spec = pl.BlockSpec((BF16_ROWS, d), lambda i: (jnp.maximum(i * per - 1, 0), 0))
    return pl.pallas_call(
        functools.partial(_inproj_conv_kernel, n_main=n_main, conv_lo=conv_lo,
                          conv_hi=conv_lo + conv_w.shape[2], tn=tn, blocks_per_seq=seq // tm),
        grid=(t // tm,),
        in_specs=[x_spec, prev_spec, _layer(norm_w.shape, layer), _layer(w.shape, wlayer),
                  _layer(conv_w.shape, wlayer), _layer(conv_b.shape, wlayer)],
        out_specs=out_specs, out_shape=out_shape,
        compiler_params=_params(("parallel",)),
        name="inproj_conv",
    )(h, h, norm_w, w, conv_w, conv_b)


def _memkv_kernel(mem_ref, nw_ref, w_ref, o_ref):
    xn = _rms(mem_ref[...], nw_ref[0]).astype(BF16)
    o_ref[0] = _dot(xn, w_ref[0].astype(BF16)).astype(BF16)


def _memkv(mem2d, mem_norm, w_kv):
    depth, d, n = w_kv.shape
    rows = mem2d.shape[0]
    return pl.pallas_call(
        _memkv_kernel,
        grid=(depth,),
        in_specs=[_resident((rows, d)),
                  pl.BlockSpec((1, 1, d), lambda i: (i, 0, 0)),
                  pl.BlockSpec((1, d, n), lambda i: (i, 0, 0))],
        out_specs=pl.BlockSpec((1, rows, n), lambda i: (i, 0, 0)),
        out_shape=jax.ShapeDtypeStruct((depth, rows, n), BF16),
        compiler_params=_params(("parallel",)),
        name="memkv",
    )(mem2d, mem_norm, w_kv)


def _gla_kernel(q_ref, k_ref, v_ref, g_ref, gl_ref, wg_ref, bg_ref, hn_ref, o_ref, st_ref):
    L = CHUNK

    @pl.when(pl.program_id(1) == 0)
    def _():
        st_ref[...] = jnp.zeros_like(st_ref)

    causal = _causal(L)
    tril = jnp.where(causal, 1.0, 0.0).astype(BF16)
    seqs = range(q_ref.shape[0])
    pre = [_dot(gl_ref[bi].astype(BF16), wg_ref[0]) + bg_ref[0] for bi in seqs]
    log2_a = [jnp.minimum(p, 0.0) * (LOG2E / GLA_GATE_TAU)
              - jnp.log2(1.0 + jnp.exp2(jnp.abs(p) * -LOG2E)) * (1.0 / GLA_GATE_TAU) for p in pre]
    b = [_cumsum_rows(tril, la) for la in log2_a]
    b_mid = [x[L // 2 - 1:L // 2, :] for x in b]
    b_last = [x[L - 1:L, :] for x in b]
    q = [q_ref[bi].astype(F32) * (GLA_HEAD_K ** -0.5) * jnp.exp2(b[bi] - b_mid[bi]) for bi in seqs]
    k = [k_ref[bi].astype(F32) * jnp.exp2(b_mid[bi] - b[bi]) for bi in seqs]
    q_mid = [x.astype(BF16) for x in q]
    k_mid = [x.astype(BF16) for x in k]
    q_dec = [(q[bi] * jnp.exp2(b_mid[bi])).astype(BF16) for bi in seqs]
    k_end = [(k[bi] * jnp.exp2(b_last[bi] - b_mid[bi])).astype(BF16) for bi in seqs]
    dec = [jnp.exp2(x) for x in b_last]
    for h in range(GLA_HEADS):
        ks = slice(h * GLA_HEAD_K, (h + 1) * GLA_HEAD_K)
        vs = slice(h * GLA_HEAD_V, (h + 1) * GLA_HEAD_V)
        s = [jnp.where(causal, _dot_nt(q_mid[bi][:, ks], k_mid[bi][:, ks]), 0.0).astype(BF16) for bi in seqs]
        st = [st_ref[bi * GLA_HEADS + h] for bi in seqs]
        o = [_dot(s[bi], v_ref[bi, :, vs]) + _dot_nt(q_dec[bi][:, ks], st[bi].astype(BF16)) for bi in seqs]
        for bi in seqs:
            st_ref[bi * GLA_HEADS + h] = st[bi] * dec[bi][:, ks] + _dot_tn(v_ref[bi, :, vs], k_end[bi][:, ks])
        for bi in seqs:
            g = g_ref[bi, :, vs].astype(F32)
            o_ref[bi, :, vs] = (_rms(o[bi], hn_ref[0]) * _silu(g)).astype(BF16)


def _gla(proj, aux, wg, bg, hnorm, layer, batch, seq):
    t, n = proj.shape
    nb = 4 if batch % 4 == 0 else 1
    proj3 = proj.reshape(batch, seq, n)
    aux3 = aux.reshape(batch, seq, LANES)
    out = pl.pallas_call(
        _gla_kernel,
        grid=(batch // nb, seq // CHUNK),
        in_specs=[pl.BlockSpec((nb, CHUNK, GLA_DK), lambda b, c: (b, c, 0)),
                  pl.BlockSpec((nb, CHUNK, GLA_DK), lambda b, c: (b, c, 1)),
                  pl.BlockSpec((nb, CHUNK, GLA_DV), lambda b, c: (b, c, 1)),
                  pl.BlockSpec((nb, CHUNK, GLA_DV), lambda b, c: (b, c, 2)),
                  pl.BlockSpec((nb, CHUNK, LANES), lambda b, c: (b, c, 0)),
                  _layer(wg.shape, layer), _layer(bg.shape, layer), _layer(hnorm.shape, layer)],
        out_specs=pl.BlockSpec((nb, CHUNK, GLA_DV), lambda b, c: (b, c, 0)),
        out_shape=jax.ShapeDtypeStruct((batch, seq, GLA_DV), BF16),
        scratch_shapes=[pltpu.VMEM((nb * GLA_HEADS, GLA_HEAD_V, GLA_HEAD_K), F32)],
        compiler_params=_params(("parallel", "arbitrary")),
        name="gla",
    )(proj3, proj3, proj3, proj3, aux3, wg, bg, hnorm)
    return out.reshape(t, GLA_DV)


def _ssd_kernel(xs_ref, bm_ref, cm_ref, dt_ref, dtb_ref, alog_ref, dsk_ref, ex_ref, o_ref, st_ref, dtx_ref):
    L = CHUNK
    P = SSD_HEAD_DIM
    N = SSD_STATE

    @pl.when(pl.program_id(1) == 0)
    def _():
        st_ref[...] = jnp.zeros_like(st_ref)

    causal = _causal(L)
    tril = jnp.where(causal, 1.0, 0.0).astype(BF16)
    a = -jnp.exp(alog_ref[0])
    lane_lo = lax.broadcasted_iota(jnp.int32, (1, LANES), 1) < P
    seqs = range(xs_ref.shape[0])
    cs2, cs2_t, last2 = [], [], []
    for bi in seqs:
        dt = _softplus(dt_ref[bi] + dtb_ref[0])
        cs2.append(_cumsum_rows(tril, dt * a) * LOG2E)
        cs2_t.append(cs2[bi].T)
        last2.append(cs2[bi][L - 1:L, :])
        dt_hi = dt.astype(BF16)
        dt_lo = (dt - dt_hi.astype(F32)).astype(BF16)
        dtx_ref[bi] = _dot(jnp.concatenate([dt_hi, dt_lo], axis=1), ex_ref[...])

    for g in range(SSD_GROUPS):
        gs = slice(g * N, (g + 1) * N)
        cb = [jnp.where(causal, _dot_nt(cm_ref[bi, :, gs], bm_ref[bi, :, gs]), 0.0) for bi in seqs]
        for e2 in range(2):
            h0 = 4 * g + 2 * e2
            ps = slice((2 * g + e2) * LANES, (2 * g + e2 + 1) * LANES)
            ss = slice(e2 * LANES, (e2 + 1) * LANES)
            for bi in seqs:
                x_f = xs_ref[bi, :, ps].astype(F32)
                xd = x_f * dtx_ref[bi, :, ps]
                xd_b = xd.astype(BF16)
                st = st_ref[bi * SSD_GROUPS + g, :, ss]
                cols = [jnp.broadcast_to(cs2[bi][:, h:h + 1], (L, LANES)) for h in (h0, h0 + 1)]
                segs = [jnp.exp2(jnp.minimum(cols[e] - cs2_t[bi][h0 + e:h0 + e + 1, :], 0.0)) for e in range(2)]
                ys = [_dot((cb[bi] * seg).astype(BF16), xd_b) for seg in segs]
                col = jnp.where(lane_lo, cols[0], cols[1])
                last = jnp.where(lane_lo, last2[bi][:, h0:h0 + 1], last2[bi][:, h0 + 1:h0 + 2])
                y = (jnp.where(lane_lo, ys[0], ys[1])
                     + _dot(cm_ref[bi, :, gs], st.astype(BF16)) * jnp.exp2(col))
                o_ref[bi, :, ps] = (y + dsk_ref[0, :, ps] * x_f).astype(BF16)
                xw = (xd * jnp.exp2(last - col)).astype(BF16)
                st_ref[bi * SSD_GROUPS + g, :, ss] = st * jnp.exp2(last) + _dot_tn(bm_ref[bi, :, gs], xw)


def _ssd(proj, aux, dt_bias, a_log, d_skip, layer, batch, seq):
    t, n = proj.shape
    nb = 4 if batch % 4 == 0 else 1
    proj3 = proj.reshape(batch, seq, n)
    aux3 = aux.reshape(batch, seq, LANES)
    head_of = jnp.arange(SSD_D_INNER, dtype=jnp.int32) // SSD_HEAD_DIM
    expand = (jnp.arange(2 * LANES, dtype=jnp.int32)[:, None] % LANES == head_of[None, :]).astype(BF16)
    out = pl.pallas_call(
        _ssd_kernel,
        grid=(batch // nb, seq // CHUNK),
        in_specs=[pl.BlockSpec((nb, CHUNK, SSD_D_INNER), lambda b, c: (b, c, 1)),
                  pl.BlockSpec((nb, CHUNK, SSD_BC), lambda b, c: (b, c, 4)),
                  pl.BlockSpec((nb, CHUNK, SSD_BC), lambda b, c: (b, c, 5)),
                  pl.BlockSpec((nb, CHUNK, LANES), lambda b, c: (b, c, 0)),
                  _layer(dt_bias.shape, layer), _layer(a_log.shape, layer), _layer(d_skip.shape, layer),
                  _resident(expand.shape)],
        out_specs=pl.BlockSpec((nb, CHUNK, SSD_D_INNER), lambda b, c: (b, c, 0)),
        out_shape=jax.ShapeDtypeStruct((batch, seq, SSD_D_INNER), BF16),
        scratch_shapes=[pltpu.VMEM((nb * SSD_GROUPS, SSD_STATE, 4 * SSD_HEAD_DIM), F32),
                        pltpu.VMEM((nb, CHUNK, SSD_D_INNER), F32)],
        compiler_params=_params(("parallel", "arbitrary")),
        name="ssd",
    )(proj3, proj3, proj3, aux3, dt_bias, a_log, d_skip, expand)
    return out.reshape(t, SSD_D_INNER)


def _xattn_proj(xq_ref, kv_ref, wo_ref, d_mix):
    heads = range(XA_HEADS)
    cols = [slice(hd * XA_HEAD_DIM, (hd + 1) * XA_HEAD_DIM) for hd in heads]
    s = [_dot_nt(xq_ref[:, c], kv_ref[0, :, c]) * (XA_HEAD_DIM ** -0.5) for c in cols]
    p = [jnp.exp(x - jnp.max(x, axis=-1, keepdims=True)) for x in s]
    pv = [_dot(p[hd].astype(BF16), kv_ref[0, :, XA_WIDTH + hd * XA_HEAD_DIM:XA_WIDTH + (hd + 1) * XA_HEAD_DIM])
          / jnp.sum(p[hd], axis=-1, keepdims=True) for hd in heads]
    acc = None
    for hd in heads:
        w = wo_ref[0, d_mix + hd * XA_HEAD_DIM:d_mix + (hd + 1) * XA_HEAD_DIM, :].astype(BF16)
        term = _dot(pv[hd].astype(BF16), w)
        acc = term if acc is None else acc + term
    return acc


def _xattn_out_kernel(mix_ref, xq_ref, kv_ref, wo_ref, h_ref, o_ref):
    d_mix = mix_ref.shape[1]
    xa = _xattn_proj(xq_ref, kv_ref, wo_ref, d_mix)
    o_ref[...] = h_ref[...] + xa + _dot(mix_ref[...], wo_ref[0, 0:d_mix, :].astype(BF16))


def _xattn_out_gated_kernel(y_ref, z_ref, nw_ref, xq_ref, kv_ref, wo_ref, h_ref, o_ref, *, tk):
    d_mix = y_ref.shape[1]
    xa = _xattn_proj(xq_ref, kv_ref, wo_ref, d_mix)
    ssq = None
    acc = None
    for c in range(d_mix // tk):
        cs = slice(c * tk, (c + 1) * tk)
        t = y_ref[:, cs].astype(F32) * _silu(z_ref[:, cs].astype(F32))
        sq = jnp.sum(t * t, axis=-1, keepdims=True)
        term = _dot((t * nw_ref[0, :, cs]).astype(BF16), wo_ref[0, cs, :].astype(BF16))
        ssq = sq if ssq is None else ssq + sq
        acc = term if acc is None else acc + term
    o_ref[...] = h_ref[...] + xa + acc * lax.rsqrt(ssq * (1.0 / d_mix) + EPS)


def _xattn_out(mix, proj, xq_block, memkv, layer, w_out, wlayer, h, seq, gate=None):
    t, d = h.shape
    tm = 1024 if gate is None and seq % 1024 == 0 else 512
    d_mix = mix.shape[1]
    mem_len = memkv.shape[1] // (t // seq)
    per_batch = seq // tm
    specs = [pl.BlockSpec((tm, XA_WIDTH), lambda i: (i, xq_block)),
             pl.BlockSpec((1, mem_len, 2 * XA_WIDTH), lambda i: (layer, i // per_batch, 0)),
             _layer(w_out.shape, wlayer),
             pl.BlockSpec((tm, d), lambda i: (i, 0))]
    mix_spec = pl.BlockSpec((tm, d_mix), lambda i: (i, 0))
    if gate is None:
        body, in_specs, args = _xattn_out_kernel, [mix_spec] + specs, (mix, proj, memkv, w_out, h)
    else:
        z_block, norm_w = gate
        body = functools.partial(_xattn_out_gated_kernel, tk=512)
        in_specs = [mix_spec, pl.BlockSpec((tm, d_mix), lambda i: (i, z_block)),
                    _layer(norm_w.shape, wlayer)] + specs
        args = (mix, proj, norm_w, proj, memkv, w_out, h)
    return pl.pallas_call(
        body,
        grid=(t // tm,),
        in_specs=in_specs,
        out_specs=pl.BlockSpec((tm, d), lambda i: (i, 0)),
        out_shape=jax.ShapeDtypeStruct((t, d), F32),
        compiler_params=_params(("parallel",)),
        name="xattn_out",
    )(*args)


def _ffn_kernel(h_ref, nw_ref, wi_ref, wo_ref, fnw_ref, o_ref, *, hidden, th, final):
    h = h_ref[...]
    xn = _rms(h, nw_ref[0]).astype(BF16)
    acc = h
    for c in range(hidden // th):
        gate = _dot(xn, wi_ref[0, :, c * th:(c + 1) * th].astype(BF16))
        up = _dot(xn, wi_ref[0, :, hidden + c * th:hidden + (c + 1) * th].astype(BF16))
        act = (_silu(gate) * up).astype(BF16)
        acc = acc + _dot(act, wo_ref[0, c * th:(c + 1) * th, :].astype(BF16))
    if final:
        acc = _rms(acc, fnw_ref[...])
    o_ref[...] = acc


def _ffn(h, norm_w, w_in, w_out, layer, final_w, final, tm=512, th=256):
    t, d = h.shape
    hidden = w_out.shape[1]
    return pl.pallas_call(
        functools.partial(_ffn_kernel, hidden=hidden, th=th, final=final),
        grid=(t // tm,),
        in_specs=[pl.BlockSpec((tm, d), lambda i: (i, 0)),
                  _layer(norm_w.shape, layer), _layer(w_in.shape, layer), _layer(w_out.shape, layer),
                  _resident((1, d))],
        out_specs=pl.BlockSpec((tm, d), lambda i: (i, 0)),
        out_shape=jax.ShapeDtypeStruct((t, d), F32),
        compiler_params=_params(("parallel",)),
        name="ffn",
    )(h, norm_w, w_in, w_out, final_w)


def _rows(v, n=None):
    v = v[:, None, :]
    return v if n is None else jnp.pad(v, ((0, 0), (0, 0), (0, n - v.shape[2])))


def _reorder_in_proj(w, n_lead, n_small):
    small = jnp.pad(w[:, :, n_lead:n_lead + n_small], ((0, 0), (0, 0), (0, LANES - n_small)))
    parts = [w[:, :, :n_lead], w[:, :, n_lead + n_small:], small]
    return jnp.concatenate([p.astype(BF16) for p in parts], axis=2)


def kernel(x, mem, mix_norm, mem_norm, w_mem_kv, gla_w_in, gla_w_gate2, gla_b_gate, gla_head_norm, gla_w_out, ssd_w_in, ssd_conv_w, ssd_conv_b, ssd_dt_bias, ssd_a_log, ssd_d, ssd_norm, ssd_w_out, ffn_norm, ffn_w_in, ffn_w_out, final_norm):
    batch, seq, d = x.shape
    depth = mix_norm.shape[0]
    t = batch * seq
    h = x.reshape(t, d)

    gla_lead = 2 * GLA_DK + 2 * GLA_DV
    ssd_lead = SSD_D_INNER + SSD_CONV_DIM
    gla_w = _reorder_in_proj(gla_w_in, gla_lead, GLA_GATE_RANK)
    ssd_w = _reorder_in_proj(ssd_w_in, ssd_lead, SSD_HEADS)
    gla_wg = jnp.pad(gla_w_gate2, ((0, 0), (0, LANES - GLA_GATE_RANK), (0, 0))).astype(BF16)
    mix_nw, ffn_nw = _rows(mix_norm), _rows(ffn_norm)
    gla_bg, gla_hn = _rows(gla_b_gate), _rows(gla_head_norm)
    ssd_cb, ssd_nw = _rows(ssd_conv_b), _rows(ssd_norm)
    ssd_dtb, ssd_alog = _rows(ssd_dt_bias, LANES), _rows(ssd_a_log, LANES)
    ssd_dsk = _rows(jnp.repeat(ssd_d, SSD_HEAD_DIM, axis=1))
    final_w = final_norm.reshape(1, d)

    memkv = _memkv(mem.reshape(-1, d), _rows(mem_norm), w_mem_kv)

    for i in range(depth):
        j = i // 2
        if i % 2 == 0:
            proj, aux = _inproj(h, mix_nw, gla_w, i, j, gla_lead + XA_WIDTH)
            mix = _gla(proj, aux, gla_wg, gla_bg, gla_hn, j, batch, seq)
            h = _xattn_out(mix, proj, gla_lead // XA_WIDTH, memkv, i, gla_w_out, j, h, seq)
        else:
            proj, aux = _inproj(h, mix_nw, ssd_w, i, j, ssd_lead + XA_WIDTH,
                                conv=(ssd_conv_w, ssd_cb, SSD_D_INNER), seq=seq)
            y = _ssd(proj, aux, ssd_dtb, ssd_alog, ssd_dsk, j, batch, seq)
            h = _xattn_out(y, proj, ssd_lead // XA_WIDTH, memkv, i, ssd_w_out, j, h, seq,
                           gate=(0, ssd_nw))
        h = _ffn(h, ffn_nw, ffn_w_in, ffn_w_out, i, final_w, final=(i == depth - 1))
    return h.reshape(batch, seq, d)
```
